```python
import math
import jax, jax.numpy as jnp
from jax import lax
import numpy as np

D_MODEL = 1024
BATCH = 2
SEQ = 8192
DEPTH = 4
DEC_BATCH = 32
DEC_SEQ = 4
PAST_LEN = 8192
PAGE_SIZE = 128

HEAD_DIM = 64
SB_HEADS = 8
SB_WIDTH = SB_HEADS * HEAD_DIM
SB_BIAS_INIT = -6.0
LRU_WIDTH = D_MODEL // 2
LRU_BLOCKS = 8
LRU_BLOCK_W = LRU_WIDTH // LRU_BLOCKS
CONV_W = 4
RGLRU_C = 8.0
DIL_HEADS = D_MODEL // HEAD_DIM
DIL_WIDTH = DIL_HEADS * HEAD_DIM
DIL_PAIRS = ((128, 1), (512, 4), (2048, 16))
DIL_MAX_WINDOW = 2048
D_FF = 4 * D_MODEL
REL_BUCKETS = 32
REL_MAX_DIST = 2048
Q_BLOCK = 128
NORM_EPS = 1e-6
N_EVEN = (DEPTH + 1) // 2
N_ODD = DEPTH // 2
AB_IN_WIDTH = 3 * SB_WIDTH + 2 * LRU_WIDTH

kernel_name = 'hybrid_sb_rglru_dilated_decode_step'


def _rms_norm(x, g):
    xf = x.astype(jnp.float32)
    y = xf * lax.rsqrt(jnp.mean(xf * xf, axis=-1, keepdims=True) + NORM_EPS)
    return (y * g.astype(jnp.float32)).astype(x.dtype)


def _sq_relu_mlp(x, w1, w2):
    return jnp.square(jax.nn.relu(x @ w1)) @ w2


def _rel_bucket(dist):
    exact = REL_BUCKETS // 2
    df = jnp.maximum(dist.astype(jnp.float32), 1.0)
    large = exact + (jnp.log(df / exact) / math.log(REL_MAX_DIST / exact)
                     * (REL_BUCKETS - exact)).astype(jnp.int32)
    large = jnp.minimum(large, REL_BUCKETS - 1)
    return jnp.where(dist < exact, dist, large)


def _stick_breaking(q, k, v, sb_bias, q_offset):
    b, tq, h, hd = q.shape
    tk = k.shape[1]
    blk = Q_BLOCK if tq % Q_BLOCK == 0 else tq
    n_blk = tq // blk
    kpos = jnp.arange(tk)
    scale = 1.0 / math.sqrt(hd)
    bias = sb_bias.astype(jnp.float32)[None, :, None, None]

    def one_block(args):
        qb, start = args
        qpos = q_offset + start + jnp.arange(blk)
        z = jnp.einsum('bqhd,bkhd->bhqk', qb, k).astype(jnp.float32) * scale + bias
        mask = kpos[None, :] < qpos[:, None]
        sp = jnp.where(mask, jax.nn.softplus(z), 0.0)
        rest = lax.cumsum(sp, axis=3, reverse=True) - sp
        w = jnp.where(mask, jnp.exp(jax.nn.log_sigmoid(z) - rest), 0.0)
        return jnp.einsum('bhqk,bkhd->bqhd', w.astype(v.dtype), v)

    qs = q.reshape(b, n_blk, blk, h, hd).swapaxes(0, 1)
    starts = jnp.arange(n_blk) * blk
    o = lax.map(one_block, (qs, starts))
    return o.swapaxes(0, 1).reshape(b, tq, h, hd)


def _gather_pages(pool, page_table):
    pages = pool[page_table]
    b, n, p, h, d = pages.shape
    return pages.reshape(b, n * p, h, d)


def _causal_conv(x, state, w, bias):
    t = x.shape[1]
    xp = jnp.concatenate([state.astype(x.dtype), x], axis=1)
    y = bias + sum(w[j] * xp[:, j:j + t] for j in range(CONV_W))
    return y, xp[:, t:]


def _lin_combine(left, right):
    a_l, h_l = left
    a_r, h_r = right
    return a_l * a_r, a_r * h_l + h_r


def _rglru(xc, h0, wa, ba, wx, bx, lam):
    b, t, w = xc.shape
    xf = xc.astype(jnp.float32)
    xb = xf.reshape(b, t, LRU_BLOCKS, LRU_BLOCK_W)
    r = jax.nn.sigmoid(jnp.einsum('btnc,ncd->btnd', xb, wa.astype(jnp.float32)).reshape(b, t, w) + ba)
    i = jax.nn.sigmoid(jnp.einsum('btnc,ncd->btnd', xb, wx.astype(jnp.float32)).reshape(b, t, w) + bx)
    log_a = -RGLRU_C * jax.nn.softplus(-lam.astype(jnp.float32)) * r
    a = jnp.exp(log_a)
    inp = jnp.sqrt(-jnp.expm1(2.0 * log_a)) * (i * xf)
    inp = inp.at[:, 0].add(a[:, 0] * h0.astype(jnp.float32))
    _, h = lax.associative_scan(_lin_combine, (a, inp), axis=1)
    return h.astype(xc.dtype), h[:, -1].astype(xc.dtype)


def _even_mixer(u, past_k, past_v, conv_state, lru_state, q_offset,
                w_in, sb_bias, cw, cb, wa, ba, wx, bx, lam, w_out):
    b, t, _ = u.shape
    z = u @ w_in
    q, k, v, xr, g = jnp.split(z, [SB_WIDTH, 2 * SB_WIDTH, 3 * SB_WIDTH,
                                   3 * SB_WIDTH + LRU_WIDTH], axis=-1)
    q = q.reshape(b, t, SB_HEADS, HEAD_DIM)
    k = k.reshape(b, t, SB_HEADS, HEAD_DIM)
    v = v.reshape(b, t, SB_HEADS, HEAD_DIM)
    attn = _stick_breaking(q, jnp.concatenate([past_k.astype(k.dtype), k], axis=1),
                           jnp.concatenate([past_v.astype(v.dtype), v], axis=1), sb_bias, q_offset)
    xc, conv_new = _causal_conv(xr, conv_state, cw, cb)
    h, h_last = _rglru(xc, lru_state, wa, ba, wx, bx, lam)
    lru_out = jax.nn.gelu(g) * h
    out = jnp.concatenate([attn.reshape(b, t, SB_WIDTH).astype(u.dtype), lru_out], axis=-1) @ w_out
    return out, k, v, conv_new, h_last


def _dil_qkv(u, w_in):
    b, t, _ = u.shape
    q, k, v = jnp.split(u @ w_in, 3, axis=-1)
    shp = (b, t, DIL_HEADS, HEAD_DIM)
    return q.reshape(shp), k.reshape(shp), v.reshape(shp)


def _dil_branch_prompt(q, k, v, rel_bias, window, dil):
    b, s, h, hd = q.shape
    n_back = window // dil
    L = s // dil
    qb_len = math.gcd(Q_BLOCK, L)
    n_blk = L // qb_len
    kw = qb_len + n_back

    def to_res(t):
        return t.reshape((b, L, dil) + t.shape[2:]).swapaxes(1, 2).reshape((b * dil, L) + t.shape[2:])

    def from_res(t):
        return t.reshape((b, dil, L) + t.shape[3:]).swapaxes(1, 2).reshape((b, s) + t.shape[3:])

    pad = ((0, 0), (n_back, 0), (0, 0), (0, 0))
    kp = jnp.pad(to_res(k), pad)
    vp = jnp.pad(to_res(v), pad)
    kidx = jnp.arange(n_blk)[:, None] * qb_len + jnp.arange(kw)[None, :]
    kb, vb = kp[:, kidx], vp[:, kidx]
    qb = to_res(q).reshape(b * dil, n_blk, qb_len, h, hd)
    logits = jnp.einsum('gnqhd,gnkhd->gnhqk', qb, kb).astype(jnp.float32) / math.sqrt(hd)
    qi = jnp.arange(qb_len)[:, None]
    kk = jnp.arange(kw)[None, :]
    j = qi + n_back - kk
    lpos = jnp.arange(n_blk)[:, None, None] * qb_len + kk[None] - n_back
    valid = ((j >= 0) & (j <= n_back))[None] & (lpos >= 0)
    bias = rel_bias[_rel_bucket(jnp.clip(j, 0, n_back) * dil)].astype(jnp.float32)
    logits = logits + bias.transpose(2, 0, 1)
    logits = jnp.where(valid[None, :, None], logits, -jnp.inf)
    m = jnp.max(logits, axis=-1)
    p = jnp.exp(logits - m[..., None])
    den = jnp.sum(p, axis=-1)
    o = jnp.einsum('gnhqk,gnkhd->gnqhd', p.astype(vb.dtype), vb) / den.transpose(0, 1, 3, 2)[..., None]
    return from_res(o), from_res(m.transpose(0, 1, 3, 2)), from_res(den.transpose(0, 1, 3, 2))


def _dil_branch_sample(q, kcat, vcat, rel_bias, window, dil, buf_len):
    b, t, h, hd = q.shape
    n_back = window // dil
    j = jnp.arange(n_back + 1)
    idx = buf_len + jnp.arange(t)[:, None] - j[None, :] * dil
    valid = idx >= 0
    idx = jnp.maximum(idx, 0)
    kg, vg = kcat[:, idx], vcat[:, idx]
    logits = jnp.einsum('bqhd,bqjhd->bqhj', q, kg).astype(jnp.float32) / math.sqrt(hd)
    logits = logits + rel_bias[_rel_bucket(j * dil)].astype(jnp.float32).T
    logits = jnp.where(valid[None, :, None, :], logits, -jnp.inf)
    m = jnp.max(logits, axis=-1)
    p = jnp.exp(logits - m[..., None])
    den = jnp.sum(p, axis=-1)
    o = jnp.einsum('bqhj,bqjhd->bqhd', p.astype(vg.dtype), vg) / den[..., None]
    return o, m, den


def _merge_branches(outs):
    m_max = jnp.max(jnp.stack([m for _, m, _ in outs]), axis=0)
    num = 0.0
    tot = 0.0
    for o, m, den in outs:
        w = den * jnp.exp(m - m_max)
        num = num + w[..., None] * o
        tot = tot + w
    return num / tot[..., None]


def _odd_mixer_prompt(u, w_in, w_out, rel_bias):
    b, t, _ = u.shape
    q, k, v = _dil_qkv(u, w_in)
    outs = [_dil_branch_prompt(q, k, v, rel_bias, win, dil) for win, dil in DIL_PAIRS]
    o = _merge_branches(outs).astype(u.dtype).reshape(b, t, DIL_WIDTH)
    keep = min(DIL_MAX_WINDOW, t)
    return o @ w_out, k[:, t - keep:], v[:, t - keep:]


def _odd_mixer_sample(u, buf_k, buf_v, w_in, w_out, rel_bias):
    b, t, _ = u.shape
    q, k, v = _dil_qkv(u, w_in)
    kcat = jnp.concatenate([buf_k.astype(k.dtype), k], axis=1)
    vcat = jnp.concatenate([buf_v.astype(v.dtype), v], axis=1)
    buf_len = buf_k.shape[1]
    outs = [_dil_branch_sample(q, kcat, vcat, rel_bias, win, dil, buf_len) for win, dil in DIL_PAIRS]
    o = _merge_branches(outs).astype(u.dtype).reshape(b, t, DIL_WIDTH)
    return o @ w_out, k, v


def setup_inputs(seed: int = 0) -> dict:
    key = jax.random.key(seed)
    ks = jax.random.split(key, 32)
    f32 = jnp.float32
    n_pages = PAST_LEN // PAGE_SIZE
    n_used = DEC_BATCH * n_pages
    n_pool = n_used + (n_used + 3) // 4
    c_buf = min(DIL_MAX_WINDOW, PAST_LEN)

    def nrm(k, shape, scale):
        return scale * jax.random.normal(k, shape, f32)

    page_table = jax.random.permutation(ks[0], n_pool)[:n_used].reshape(DEC_BATCH, n_pages).astype(jnp.int32)
    a0 = jax.random.uniform(ks[20], (N_EVEN, LRU_WIDTH), f32, 0.9, 0.999)
    s0 = a0 ** (1.0 / RGLRU_C)
    lru_lambda = jnp.log(s0) - jnp.log1p(-s0)
    return {
        'x_prompt': nrm(ks[1], (BATCH, SEQ, D_MODEL), 1.0),
        'x_sample': nrm(ks[2], (DEC_BATCH, DEC_SEQ, D_MODEL), 1.0),
        'cache_sb_k': nrm(ks[3], (N_EVEN, n_pool, PAGE_SIZE, SB_HEADS, HEAD_DIM), 1.0),
        'cache_sb_v': nrm(ks[4], (N_EVEN, n_pool, PAGE_SIZE, SB_HEADS, HEAD_DIM), 1.0),
        'state_conv': nrm(ks[5], (N_EVEN, DEC_BATCH, CONV_W - 1, LRU_WIDTH), 1.0),
        'state_lru': nrm(ks[6], (N_EVEN, DEC_BATCH, LRU_WIDTH), 0.5),
        'cache_dil_k': nrm(ks[7], (N_ODD, DEC_BATCH, c_buf, DIL_HEADS, HEAD_DIM), 1.0),
        'cache_dil_v': nrm(ks[8], (N_ODD, DEC_BATCH, c_buf, DIL_HEADS, HEAD_DIM), 1.0),
        'page_table': page_table,
        'rel_bias': nrm(ks[9], (REL_BUCKETS, DIL_HEADS), 0.5),
        'norm_mix': 1.0 + nrm(ks[10], (DEPTH, D_MODEL), 0.02),
        'norm_ffn': 1.0 + nrm(ks[11], (DEPTH, D_MODEL), 0.02),
        'norm_final': 1.0 + nrm(ks[12], (D_MODEL,), 0.02),
        'w_in_ab': nrm(ks[13], (N_EVEN, D_MODEL, AB_IN_WIDTH), D_MODEL ** -0.5),
        'sb_bias': SB_BIAS_INIT + nrm(ks[26], (N_EVEN, SB_HEADS), 0.5),
        'conv_w': nrm(ks[14], (N_EVEN, CONV_W, LRU_WIDTH), CONV_W ** -0.5),
        'conv_b': nrm(ks[15], (N_EVEN, LRU_WIDTH), 0.01),
        'lru_wa': nrm(ks[16], (N_EVEN, LRU_BLOCKS, LRU_BLOCK_W, LRU_BLOCK_W), LRU_BLOCK_W ** -0.5),
        'lru_ba': nrm(ks[17], (N_EVEN, LRU_WIDTH), 0.01),
        'lru_wx': nrm(ks[18], (N_EVEN, LRU_BLOCKS, LRU_BLOCK_W, LRU_BLOCK_W), LRU_BLOCK_W ** -0.5),
        'lru_bx': nrm(ks[19], (N_EVEN, LRU_WIDTH), 0.01),
        'lru_lambda': lru_lambda,
        'w_out_ab': nrm(ks[21], (N_EVEN, SB_WIDTH + LRU_WIDTH, D_MODEL), (SB_WIDTH + LRU_WIDTH) ** -0.5),
        'w_in_c': nrm(ks[22], (N_ODD, D_MODEL, 3 * DIL_WIDTH), D_MODEL ** -0.5),
        'w_out_c': nrm(ks[23], (N_ODD, DIL_WIDTH, D_MODEL), DIL_WIDTH ** -0.5),
        'w_ff1': nrm(ks[24], (DEPTH, D_MODEL, D_FF), D_MODEL ** -0.5),
        'w_ff2': nrm(ks[25], (DEPTH, D_FF, D_MODEL), D_FF ** -0.5),
    }


def reference(x_prompt, x_sample, cache_sb_k, cache_sb_v, state_conv, state_lru,
              cache_dil_k, cache_dil_v, page_table, rel_bias, norm_mix, norm_ffn,
              norm_final, w_in_ab, sb_bias, conv_w, conv_b, lru_wa, lru_ba, lru_wx, lru_bx,
              lru_lambda, w_out_ab, w_in_c, w_out_c, w_ff1, w_ff2):
    n_pages = page_table.shape[1]
    past_len = n_pages * cache_sb_k.shape[2]
    b_p = x_prompt.shape[0]
    hp, hs = x_prompt, x_sample
    sb_k_p, sb_v_p, sb_k_s, sb_v_s = [], [], [], []
    conv_p, conv_s, lru_p, lru_s = [], [], [], []
    dk_p, dv_p, dk_s, dv_s = [], [], [], []
    for layer in range(DEPTH):
        i = layer // 2
        up = _rms_norm(hp, norm_mix[layer])
        us = _rms_norm(hs, norm_mix[layer])
        if layer % 2 == 0:
            ab = (w_in_ab[i], sb_bias[i], conv_w[i], conv_b[i], lru_wa[i], lru_ba[i],
                  lru_wx[i], lru_bx[i], lru_lambda[i], w_out_ab[i])
            no_past = jnp.zeros((b_p, 0, SB_HEADS, HEAD_DIM), hp.dtype)
            mp, kp_, vp_, cp_, lp_ = _even_mixer(
                up, no_past, no_past,
                jnp.zeros((b_p, CONV_W - 1, LRU_WIDTH), hp.dtype),
                jnp.zeros((b_p, LRU_WIDTH), hp.dtype), 0, *ab)
            ms, ks_, vs_, cs_, ls_ = _even_mixer(
                us, _gather_pages(cache_sb_k[i], page_table),
                _gather_pages(cache_sb_v[i], page_table),
                state_conv[i], state_lru[i], past_len, *ab)
            sb_k_p.append(kp_); sb_v_p.append(vp_); sb_k_s.append(ks_); sb_v_s.append(vs_)
            conv_p.append(cp_); conv_s.append(cs_); lru_p.append(lp_); lru_s.append(ls_)
        else:
            mp, kp_, vp_ = _odd_mixer_prompt(up, w_in_c[i], w_out_c[i], rel_bias)
            ms, ks_, vs_ = _odd_mixer_sample(us, cache_dil_k[i], cache_dil_v[i],
                                             w_in_c[i], w_out_c[i], rel_bias)
            dk_p.append(kp_); dv_p.append(vp_); dk_s.append(ks_); dv_s.append(vs_)
        hp = hp + mp
        hs = hs + ms
        hp = hp + _sq_relu_mlp(_rms_norm(hp, norm_ffn[layer]), w_ff1[layer], w_ff2[layer])
        hs = hs + _sq_relu_mlp(_rms_norm(hs, norm_ffn[layer]), w_ff1[layer], w_ff2[layer])
    y_prompt = _rms_norm(hp, norm_final)
    y_sample = _rms_norm(hs, norm_final)
    sb_k_prompt = jnp.stack(sb_k_p)
    sb_v_prompt = jnp.stack(sb_v_p)
    sb_k_sample = jnp.stack(sb_k_s)
    sb_v_sample = jnp.stack(sb_v_s)
    conv_prompt = jnp.stack(conv_p)
    conv_sample = jnp.stack(conv_s)
    lru_prompt = jnp.stack(lru_p)
    lru_sample = jnp.stack(lru_s)
    dil_k_prompt = jnp.stack(dk_p)
    dil_v_prompt = jnp.stack(dv_p)
    dil_k_sample = jnp.stack(dk_s)
    dil_v_sample = jnp.stack(dv_s)
    return (y_prompt, y_sample, sb_k_prompt, sb_v_prompt, sb_k_sample, sb_v_sample,
            conv_prompt, conv_sample, lru_prompt, lru_sample,
            dil_k_prompt, dil_v_prompt, dil_k_sample, dil_v_sample)
```

```python
import functools
import math

import numpy as np
import jax
import jax.numpy as jnp
from jax import lax
from jax.experimental import pallas as pl
from jax.experimental.pallas import tpu as pltpu

F32 = jnp.float32
BF16 = jnp.bfloat16

HEAD_DIM = 64
SB_HEADS = 8
LRU_BLOCKS = 8
CONV_W = 4
RGLRU_C = 8.0
DIL_PAIRS = ((128, 1), (512, 4), (2048, 16))
DIL_MAX_WINDOW = 2048
DIL_BACK = 128
REL_BUCKETS = 32
REL_MAX_DIST = 2048
NORM_EPS = 1e-6
NEG = -1e30
LANES = 128
HEAD_GROUP = 4
GROUP_W = HEAD_GROUP * HEAD_DIM
VMEM_LIMIT = 56 * 1024 * 1024


def _cparams(*sem):
    return pltpu.CompilerParams(dimension_semantics=sem, vmem_limit_bytes=VMEM_LIMIT)


def _softplus(z):
    return jnp.maximum(z, 0.0) + jnp.log1p(jnp.exp(-jnp.abs(z)))


def _sigmoid(z):
    return 1.0 / (1.0 + jnp.exp(-z))


def _gelu_tanh(x):
    c = math.sqrt(2.0 / math.pi)
    return 0.5 * x * (1.0 + jnp.tanh(c * (x + 0.044715 * (x * x * x))))


def _rms(x, g):
    inv = lax.rsqrt(jnp.mean(x * x, axis=-1, keepdims=True) + NORM_EPS)
    return x * inv * g


def _dot(a, b):
    return jnp.dot(a, b, preferred_element_type=F32)


def _dot_nt(a, b):
    return lax.dot_general(a, b, (((1,), (1,)), ((), ())), preferred_element_type=F32)


def _pad_rows_bf16(x, rows):
    pad = jnp.zeros((rows - x.shape[0], x.shape[1]), F32)
    return jnp.concatenate([x, pad], axis=0).astype(BF16)


def _split_hi_lo(x):
    hi = x.astype(BF16)
    lo = (x - hi.astype(F32)).astype(BF16)
    return jnp.concatenate([hi, lo], axis=1)


def _suffix_matrix():
    j = np.arange(LANES)[:, None]
    s = np.arange(LANES)[None, :]
    half = np.concatenate([(j > s).astype(np.float32), np.ones((LANES, LANES), np.float32)], axis=1)
    return jnp.asarray(np.concatenate([half, half], axis=0), dtype=BF16)


def _norm_proj_kernel(x_ref, g_ref, w_ref, *out_refs, outs):
    xn = _rms(x_ref[...], g_ref[...]).astype(BF16)
    cache = {}
    for o_ref, (off, width, scale) in zip(out_refs, outs):
        if (off, width) not in cache:
            cache[off, width] = _dot(xn, w_ref[:, off:off + width])
        z = cache[off, width]
        if scale != 1.0:
            z = z * scale
        o_ref[...] = z.astype(o_ref.dtype)


def _norm_proj(x, g, w, outs, dtypes, tm):
    m, d = x.shape
    n = w.shape[1]
    tm = min(tm, m)
    kern = functools.partial(_norm_proj_kernel, outs=tuple(outs))
    return pl.pallas_call(
        kern,
        grid=(m // tm,),
        in_specs=[pl.BlockSpec((tm, d), lambda i: (i, 0)),
                  pl.BlockSpec((1, d), lambda i: (0, 0)),
                  pl.BlockSpec((d, n), lambda i: (0, 0))],
        out_specs=[pl.BlockSpec((tm, wd), lambda i: (i, 0)) for (_, wd, _) in outs],
        out_shape=[jax.ShapeDtypeStruct((m, wd), dt) for (_, wd, _), dt in zip(outs, dtypes)],
        compiler_params=_cparams("parallel"),
        name="norm_proj",
    )(x, g.reshape(1, d), w)


def _out_mlp_kernel(*refs, n_parts, final):
    h_ref = refs[0]
    part_refs = refs[1:1 + n_parts]
    wo_refs = refs[1 + n_parts:1 + 2 * n_parts]
    rest = refs[1 + 2 * n_parts:]
    if final:
        gf_ref, w1_ref, w2_ref, gfin_ref, o_ref, hn_ref, xn_ref, acc_ref = rest
    else:
        gf_ref, w1_ref, w2_ref, o_ref, hn_ref, xn_ref, acc_ref = rest
    f = pl.program_id(1)

    @pl.when(f == 0)
    def _():
        hn = h_ref[...]
        for p_ref, wo_ref in zip(part_refs, wo_refs):
            hn = hn + _dot(p_ref[...].astype(BF16), wo_ref[...])
        hn_ref[...] = hn
        xn_ref[...] = _rms(hn, gf_ref[...]).astype(BF16)
        acc_ref[...] = jnp.zeros_like(acc_ref)

    u = jnp.maximum(_dot(xn_ref[...], w1_ref[...]), 0.0)
    acc_ref[...] += _dot((u * u).astype(BF16), w2_ref[...])

    @pl.when(f == pl.num_programs(1) - 1)
    def _():
        out = hn_ref[...] + acc_ref[...]
        if final:
            out = _rms(out, gfin_ref[...])
        o_ref[...] = out


def _out_mlp(h, parts, wouts, g_ffn, w1, w2, g_final, tm, tf):
    m, d = h.shape
    dff = w1.shape[1]
    tm = min(tm, m)
    final = g_final is not None
    n_parts = len(parts)
    in_specs = [pl.BlockSpec((tm, d), lambda i, f: (i, 0))]
    in_specs += [pl.BlockSpec((tm, p.shape[1]), lambda i, f: (i, 0)) for p in parts]
    in_specs += [pl.BlockSpec(wo.shape, lambda i, f: (0, 0)) for wo in wouts]
    in_specs += [pl.BlockSpec((1, d), lambda i, f: (0, 0)),
                 pl.BlockSpec((d, tf), lambda i, f: (0, f)),
                 pl.BlockSpec((tf, d), lambda i, f: (f, 0))]
    args = [h, *parts, *wouts, g_ffn.reshape(1, d), w1, w2]
    if final:
        in_specs.append(pl.BlockSpec((1, d), lambda i, f: (0, 0)))
        args.append(g_final.reshape(1, d))
    kern = functools.partial(_out_mlp_kernel, n_parts=n_parts, final=final)
    return pl.pallas_call(
        kern,
        grid=(m // tm, dff // tf),
        in_specs=in_specs,
        out_specs=pl.BlockSpec((tm, d), lambda i, f: (i, 0)),
        out_shape=jax.ShapeDtypeStruct((m, d), F32),
        scratch_shapes=[pltpu.VMEM((tm, d), F32), pltpu.VMEM((tm, d), BF16), pltpu.VMEM((tm, d), F32)],
        compiler_params=_cparams("parallel", "arbitrary"),
        name="out_mlp",
    )(*args)


def _head_lane_mask(shape, h):
    lane = lax.broadcasted_iota(jnp.int32, shape, len(shape) - 1)
    return (lane >= h * HEAD_DIM) & (lane < (h + 1) * HEAD_DIM)


def _sb_block(z, r_prev, tri):
    sp = _softplus(z)
    rr = _dot(_split_hi_lo(sp), tri)
    w = jnp.exp(z - sp - rr[:, :LANES] - r_prev)
    return w, r_prev + rr[:, LANES:]


def _sb_prompt_kernel(it_ref, jt_ref, q_ref, k_ref, v_ref, bias_ref, tri_ref, o_ref,
                      qm_ref, r_ref, acc_ref, *, t):
    g = pl.program_id(1)
    p = pl.program_id(2)
    i = it_ref[p]
    j = jt_ref[p]

    @pl.when(j == i)
    def _():
        q = q_ref[...]
        for h in range(HEAD_GROUP):
            qm_ref[h] = jnp.where(_head_lane_mask(q.shape, h), q, jnp.zeros_like(q))
        r_ref[...] = jnp.zeros_like(r_ref)
        acc_ref[...] = jnp.zeros_like(acc_ref)

    def run(diag):
        def head_body(h, carry):
            s = _dot_nt(qm_ref[h], k_ref[...])
            bias = bias_ref[pl.ds(g * HEAD_GROUP + h, 1), :]
            r = r_ref[h]
            ws = [None] * (t // LANES)
            for c in reversed(range(t // LANES)):
                z = s[:, c * LANES:(c + 1) * LANES] + bias
                if diag:
                    row = lax.broadcasted_iota(jnp.int32, z.shape, 0)
                    col = lax.broadcasted_iota(jnp.int32, z.shape, 1) + c * LANES
                    z = jnp.where(col < row, z, NEG)
                w, r = _sb_block(z, r, tri_ref[...])
                ws[c] = w.astype(BF16)
            r_ref[h] = r
            acc_ref[h] += _dot(jnp.concatenate(ws, axis=1), v_ref[...])
            return carry
        lax.fori_loop(0, HEAD_GROUP, head_body, 0)

    @pl.when(j == i)
    def _():
        run(True)

    @pl.when(j != i)
    def _():
        run(False)

    @pl.when(j == 0)
    def _():
        out = jnp.zeros(o_ref.shape, F32)
        for h in range(HEAD_GROUP):
            a = acc_ref[h]
            out = jnp.where(_head_lane_mask(a.shape, h), a, out)
        o_ref[...] = out.astype(o_ref.dtype)


def _sb_prompt(q, k, v, bias_rows, tri, t):
    b, s, w = q.shape
    t = min(t, s)
    nq = s // t
    it = np.concatenate([np.full(i + 1, i) for i in range(nq)]).astype(np.int32)
    jt = np.concatenate([np.arange(i, -1, -1) for i in range(nq)]).astype(np.int32)
    grid_spec = pltpu.PrefetchScalarGridSpec(
        num_scalar_prefetch=2,
        grid=(b, w // GROUP_W, len(it)),
        in_specs=[pl.BlockSpec((None, t, GROUP_W), lambda b_, g, p, it_, jt_: (b_, it_[p], g)),
                  pl.BlockSpec((None, t, GROUP_W), lambda b_, g, p, it_, jt_: (b_, jt_[p], g)),
                  pl.BlockSpec((None, t, GROUP_W), lambda b_, g, p, it_, jt_: (b_, jt_[p], g)),
                  pl.BlockSpec(bias_rows.shape, lambda b_, g, p, it_, jt_: (0, 0)),
                  pl.BlockSpec(tri.shape, lambda b_, g, p, it_, jt_: (0, 0))],
        out_specs=pl.BlockSpec((None, t, GROUP_W), lambda b_, g, p, it_, jt_: (b_, it_[p], g)),
        scratch_shapes=[pltpu.VMEM((HEAD_GROUP, t, GROUP_W), BF16),
                        pltpu.VMEM((HEAD_GROUP, t, LANES), F32),
                        pltpu.VMEM((HEAD_GROUP, t, GROUP_W), F32)])
    return pl.pallas_call(
        functools.partial(_sb_prompt_kernel, t=t),
        grid_spec=grid_spec,
        out_shape=jax.ShapeDtypeStruct((b, s, w), BF16),
        compiler_params=_cparams("parallel", "parallel", "arbitrary"),
        name="sb_prompt",
    )(jnp.asarray(it), jnp.asarray(jt), q, k, v, bias_rows, tri)


def _block_diag_queries(q, heads, halves):
    b, t, w = q.shape
    hh = heads // halves
    wh = w // halves
    qs = (q * (1.0 / math.sqrt(HEAD_DIM))).reshape(b, t, halves, 1, wh)
    lane_head = (np.arange(wh) // HEAD_DIM)[None, :]
    mask = jnp.asarray(lane_head == np.arange(hh)[:, None])
    qbd = jnp.where(mask[None, None, None], qs, 0.0)
    return qbd.transpose(0, 2, 1, 3, 4).reshape(b, halves, t * hh, wh).astype(BF16)


def _pad_new_rows(x):
    t = x.shape[1]
    return jnp.pad(x, ((0, 0), (0, -t % 8), (0, 0)))


def _store_head_rows(o_ref, x, n_new):
    heads = x.shape[0] // n_new
    shape = (heads, x.shape[1])
    lane_head = jnp.right_shift(lax.broadcasted_iota(jnp.int32, shape, 1), HEAD_DIM.bit_length() - 1)
    sel = lane_head == lax.broadcasted_iota(jnp.int32, shape, 0)
    for tk in range(n_new):
        row = jnp.sum(jnp.where(sel, x[tk * heads:(tk + 1) * heads], 0.0), axis=0, keepdims=True)
        o_ref[tk:tk + 1, :] = row.astype(o_ref.dtype)


def _sb_sample_kernel(*refs, n_slots, n_new):
    pt_ref, qbd_ref, kn_ref, vn_ref, bias_ref, tri_ref = refs[:6]
    kp_refs = refs[6:6 + n_slots]
    vp_refs = refs[6 + n_slots:6 + 2 * n_slots]
    o_ref, r_ref, acc_ref = refs[6 + 2 * n_slots:]
    del pt_ref
    c = pl.program_id(1)
    rows, w = qbd_ref.shape
    heads = rows // n_new
    qbd = qbd_ref[...]
    bias = bias_ref[...]
    tri = tri_ref[...]

    @pl.when(c == 0)
    def _():
        z = _dot_nt(qbd, _pad_rows_bf16(kn_ref[...], LANES)) + bias
        tok = jnp.right_shift(lax.broadcasted_iota(jnp.int32, z.shape, 0), heads.bit_length() - 1)
        z = jnp.where(lax.broadcasted_iota(jnp.int32, z.shape, 1) < tok, z, NEG)
        wgt, r = _sb_block(z, jnp.zeros((rows, LANES), F32), tri)
        r_ref[...] = r
        acc_ref[...] = _dot(wgt.astype(BF16), _pad_rows_bf16(vn_ref[...], LANES))

    r = r_ref[...]
    acc = acc_ref[...]
    for kp_ref, vp_ref in zip(kp_refs, vp_refs):
        z = _dot(qbd, kp_ref[...].reshape(w, LANES).astype(BF16)) + bias
        wgt, r = _sb_block(z, r, tri)
        acc = acc + _dot_nt(wgt.astype(BF16), vp_ref[...].reshape(w, LANES).astype(BF16))
    r_ref[...] = r
    acc_ref[...] = acc

    @pl.when(c == pl.num_programs(1) - 1)
    def _():
        _store_head_rows(o_ref, acc, n_new)


def _sb_sample(q, k_new, v_new, pool_k, pool_v, layer, page_table, sb_bias, tri, n_slots):
    b, t, w = q.shape
    heads, hd, page = pool_k.shape[2:]
    n_pages = page_table.shape[1]
    assert page == LANES and n_pages % n_slots == 0 and heads & (heads - 1) == 0
    n_chunks = n_pages // n_slots
    rows = t * heads
    qbd = _block_diag_queries(q, heads, 1).reshape(b, rows, w)
    kn, vn = _pad_new_rows(k_new), _pad_new_rows(v_new)
    n_pad = kn.shape[1]
    bias_col = jnp.broadcast_to(jnp.tile(sb_bias.astype(F32), t)[:, None], (rows, LANES))

    def pool_spec(slot):
        return pl.BlockSpec(
            (None, None, heads, hd, page),
            lambda b_, c, pt: (layer, pt[b_, n_pages - 1 - (c * n_slots + slot)], 0, 0, 0))

    new_spec = pl.BlockSpec((None, n_pad, w), lambda b_, c, pt: (b_, 0, 0))
    grid_spec = pltpu.PrefetchScalarGridSpec(
        num_scalar_prefetch=1,
        grid=(b, n_chunks),
        in_specs=[pl.BlockSpec((None, rows, w), lambda b_, c, pt: (b_, 0, 0)), new_spec, new_spec,
                  pl.BlockSpec(bias_col.shape, lambda b_, c, pt: (0, 0)),
                  pl.BlockSpec(tri.shape, lambda b_, c, pt: (0, 0))]
                 + [pool_spec(sl) for sl in range(n_slots)] * 2,
        out_specs=pl.BlockSpec((None, t, w), lambda b_, c, pt: (b_, 0, 0)),
        scratch_shapes=[pltpu.VMEM((rows, LANES), F32), pltpu.VMEM((rows, w), F32)])
    out = pl.pallas_call(
        functools.partial(_sb_sample_kernel, n_slots=n_slots, n_new=t),
        grid_spec=grid_spec,
        out_shape=jax.ShapeDtypeStruct((b, t, w), F32),
        compiler_params=_cparams("parallel", "arbitrary"),
        name="sb_sample",
    )(page_table, qbd, kn, vn, bias_col, tri, *([pool_k] * n_slots), *([pool_v] * n_slots))
    return out.reshape(b * t, w)


def _lru_gates(xc, wa_ref, ba_ref, wx_ref, bx_ref, lam_ref):
    xcb = xc.astype(BF16)
    r = _sigmoid(_dot(xcb, wa_ref[...]) + ba_ref[...])
    ig = _sigmoid(_dot(xcb, wx_ref[...]) + bx_ref[...])
    log_a = (-RGLRU_C * _softplus(-lam_ref[...])) * r
    a = jnp.exp(log_a)
    inp = jnp.sqrt(-jnp.tanh(log_a) * (a * a + 1.0)) * (ig * xc)
    return a, inp


def _lru_prompt_kernel(xr_ref, g_ref, cw_ref, cb_ref, wa_ref, ba_ref, wx_ref, bx_ref, lam_ref,
                       o_ref, hl_ref, xp_ref, h_ref, *, ts):
    c = pl.program_id(1)
    pad = xp_ref.shape[0] - ts

    @pl.when(c == 0)
    def _():
        xp_ref[0:pad, :] = jnp.zeros((pad, xp_ref.shape[1]), F32)
        h_ref[...] = jnp.zeros_like(h_ref)

    x = xr_ref[...]
    xp_ref[pad:pad + ts, :] = x
    xc = cb_ref[...]
    for jj in range(CONV_W):
        sh = CONV_W - 1 - jj
        xc = xc + cw_ref[jj:jj + 1, :] * xp_ref[pad - sh:pad - sh + ts, :]
    xp_ref[0:pad, :] = x[ts - pad:ts, :]

    a, bt = _lru_gates(xc, wa_ref, ba_ref, wx_ref, bx_ref, lam_ref)
    row = lax.broadcasted_iota(jnp.int32, a.shape, 0)
    d = 1
    while d < ts:
        keep = row >= d
        bt = jnp.where(keep, a * pltpu.roll(bt, d, 0) + bt, bt)
        a = jnp.where(keep, a * pltpu.roll(a, d, 0), a)
        d *= 2
    hs = a * h_ref[...] + bt
    h_ref[...] = hs[ts - 1:ts, :]
    hl_ref[...] = hs[ts - 1:ts, :]
    o_ref[...] = (_gelu_tanh(g_ref[...]) * hs).astype(o_ref.dtype)


def _lru_prompt(xr, g, cw, cb, wa, ba, wx, bx, lam, ts):
    b, s, w = xr.shape
    ts = min(ts, s)
    row = lambda a: a.reshape(1, w)
    vec = pl.BlockSpec((1, w), lambda b_, c: (0, 0))
    mat = pl.BlockSpec((w, w), lambda b_, c: (0, 0))
    seq = pl.BlockSpec((None, ts, w), lambda b_, c: (b_, c, 0))
    return pl.pallas_call(
        functools.partial(_lru_prompt_kernel, ts=ts),
        grid=(b, s // ts),
        in_specs=[seq, seq, pl.BlockSpec((CONV_W, w), lambda b_, c: (0, 0)), vec, mat, vec, mat, vec, vec],
        out_specs=[seq, pl.BlockSpec((None, 1, w), lambda b_, c: (b_, 0, 0))],
        out_shape=[jax.ShapeDtypeStruct((b, s, w), BF16), jax.ShapeDtypeStruct((b, 1, w), F32)],
        scratch_shapes=[pltpu.VMEM((ts + 8, w), F32), pltpu.VMEM((1, w), F32)],
        compiler_params=_cparams("parallel", "arbitrary"),
        name="lru_prompt",
    )(xr, g, cw, row(cb), wa, row(ba), wx, row(bx), row(lam))


def _lru_sample_kernel(xr_ref, g_ref, st_ref, h0_ref, cw_ref, cb_ref, wa_ref, ba_ref, wx_ref, bx_ref,
                       lam_ref, o_ref, hl_ref):
    n_t = xr_ref.shape[0]
    n_s = st_ref.shape[0]
    xs = [st_ref[i] for i in range(n_s)] + [xr_ref[i] for i in range(n_t)]
    h = h0_ref[...]
    for tt in range(n_t):
        xc = cb_ref[...]
        for jj in range(CONV_W):
            xc = xc + cw_ref[jj:jj + 1, :] * xs[tt + jj]
        a, inp = _lru_gates(xc, wa_ref, ba_ref, wx_ref, bx_ref, lam_ref)
        h = a * h + inp
        o_ref[tt] = (_gelu_tanh(g_ref[tt]) * h).astype(o_ref.dtype)
    hl_ref[...] = h


def _lru_sample(xr_t, g_t, state_t, h0, cw, cb, wa, ba, wx, bx, lam):
    n_t, b, w = xr_t.shape
    row = lambda a: a.reshape(1, w)
    return pl.pallas_call(
        _lru_sample_kernel,
        out_shape=[jax.ShapeDtypeStruct((n_t, b, w), BF16), jax.ShapeDtypeStruct((b, w), F32)],
        compiler_params=pltpu.CompilerParams(vmem_limit_bytes=VMEM_LIMIT),
        name="lru_sample",
    )(xr_t, g_t, state_t, h0, cw, row(cb), wa, row(ba), wx, row(bx), row(lam))


def _rel_bucket_np(dist):
    exact = REL_BUCKETS // 2
    df = np.maximum(dist.astype(np.float64), 1.0)
    large = exact + (np.log(df / exact) / math.log(REL_MAX_DIST / exact) * (REL_BUCKETS - exact)).astype(np.int64)
    large = np.minimum(large, REL_BUCKETS - 1)
    return np.where(dist < exact, dist, large)


def _dil_prompt_kernel(q_ref, kp_ref, kc_ref, vp_ref, vc_ref, bias_ref, o_ref, m_ref, l_ref):
    tq = q_ref.shape[0]
    n_groups = q_ref.shape[1] // GROUP_W
    kcol = lax.broadcasted_iota(jnp.int32, (tq, 2 * tq), 1)
    prev_pen = jnp.where(kcol < tq, jnp.where(pl.program_id(2) == 0, NEG, 0.0), 0.0)
    lane = lax.broadcasted_iota(jnp.int32, (tq, LANES), 1)
    m_all = jnp.zeros((tq, LANES), F32)
    l_all = jnp.ones((tq, LANES), F32)
    for g in range(n_groups):
        cols = slice(g * GROUP_W, (g + 1) * GROUP_W)
        q = q_ref[:, cols]
        kk = jnp.concatenate([kp_ref[:, cols], kc_ref[:, cols]], axis=0)
        vv = jnp.concatenate([vp_ref[:, cols], vc_ref[:, cols]], axis=0)
        out = jnp.zeros((tq, GROUP_W), F32)
        for h in range(HEAD_GROUP):
            hm = _head_lane_mask(q.shape, h)
            s = _dot_nt(jnp.where(hm, q, jnp.zeros_like(q)), kk) + bias_ref[g * HEAD_GROUP + h] + prev_pen
            m = jnp.max(s, axis=1, keepdims=True)
            p = jnp.exp(s - m)
            l = jnp.sum(p, axis=1, keepdims=True)
            out = jnp.where(hm, _dot(p.astype(BF16), vv), out)
            hh = g * HEAD_GROUP + h
            m_all = jnp.where(lane == hh, m, m_all)
            l_all = jnp.where(lane == hh, l, l_all)
        o_ref[:, cols] = out
    m_ref[...] = m_all
    l_ref[...] = l_all


def _dil_prompt_branch(q, k, v, bias_tab, dil):
    b, s, w = q.shape
    tq = DIL_BACK
    length = s // dil
    n_blk = length // tq
    res = lambda a: a.reshape(b, length, dil * a.shape[2])
    blk = lambda width, prev: pl.BlockSpec(
        (None, tq, width),
        (lambda b_, r, i: (b_, jnp.maximum(i - 1, 0), r)) if prev else (lambda b_, r, i: (b_, i, r)))
    o, m, l = pl.pallas_call(
        _dil_prompt_kernel,
        grid=(b, dil, n_blk),
        in_specs=[blk(w, False), blk(w, True), blk(w, False), blk(w, True), blk(w, False),
                  pl.BlockSpec(bias_tab.shape, lambda b_, r, i: (0, 0, 0))],
        out_specs=[blk(w, False), blk(LANES, False), blk(LANES, False)],
        out_shape=[jax.ShapeDtypeStruct((b, length, dil * w), F32),
                   jax.ShapeDtypeStruct((b, length, dil * LANES), F32),
                   jax.ShapeDtypeStruct((b, length, dil * LANES), F32)],
        compiler_params=_cparams("parallel", "parallel", "arbitrary"),
        name="dil_prompt_d%d" % dil,
    )(res(q), res(k), res(k), res(v), res(v), bias_tab)
    return o.reshape(b, s, w), m.reshape(b, s, LANES), l.reshape(b, s, LANES)


def _expand_heads(x, e_ref):
    hi = x.astype(BF16)
    r1 = x - hi.astype(F32)
    mid = r1.astype(BF16)
    lo = (r1 - mid.astype(F32)).astype(BF16)
    e = e_ref[...]
    return _dot(hi, e) + _dot(mid, e) + _dot(lo, e)


def _dil_merge_kernel(*refs, n_br):
    o_refs = refs[:n_br]
    m_refs = refs[n_br:2 * n_br]
    l_refs = refs[2 * n_br:3 * n_br]
    e_ref, out_ref = refs[3 * n_br:]
    ms = [r[...] for r in m_refs]
    m_max = functools.reduce(jnp.maximum, ms)
    ws = [jnp.exp(m - m_max) for m in ms]
    tot = functools.reduce(lambda x, y: x + y, [w * r[...] for w, r in zip(ws, l_refs)])
    inv = 1.0 / tot
    acc = None
    for w, o_ref in zip(ws, o_refs):
        term = _expand_heads(w * inv, e_ref) * o_ref[...]
        acc = term if acc is None else acc + term
    out_ref[...] = acc.astype(out_ref.dtype)


def _dil_merge(outs, expand, tm):
    n_br = len(outs)
    b, s, w = outs[0][0].shape
    mtot = b * s
    tm = min(tm, mtot)
    flat = lambda a: a.reshape(mtot, a.shape[2])
    os_ = [flat(o) for o, _, _ in outs]
    ms_ = [flat(m) for _, m, _ in outs]
    ls_ = [flat(l) for _, _, l in outs]
    wide = pl.BlockSpec((tm, w), lambda i: (i, 0))
    thin = pl.BlockSpec((tm, LANES), lambda i: (i, 0))
    out = pl.pallas_call(
        functools.partial(_dil_merge_kernel, n_br=n_br),
        grid=(mtot // tm,),
        in_specs=[wide] * n_br + [thin] * (2 * n_br) + [pl.BlockSpec(expand.shape, lambda i: (0, 0))],
        out_specs=wide,
        out_shape=jax.ShapeDtypeStruct((mtot, w), BF16),
        compiler_params=_cparams("parallel"),
        name="dil_merge",
    )(*os_, *ms_, *ls_, expand)
    return out


def _dil_sample_kernel(qbd_ref, kb_ref, vb_ref, kn_ref, vn_ref, bias_ref, o_ref, *, n_new):
    wh, c_buf = kb_ref.shape[0] * kb_ref.shape[1], kb_ref.shape[2]
    qbd = qbd_ref[...]
    s = _dot(qbd, kb_ref[...].reshape(wh, c_buf).astype(BF16))
    sn = _dot_nt(qbd, _pad_rows_bf16(kn_ref[...], LANES))
    z = jnp.concatenate([s, sn], axis=1)
    logits = [z + bias_ref[i] for i in range(bias_ref.shape[0])]
    m = functools.reduce(jnp.maximum, [jnp.max(x, axis=1, keepdims=True) for x in logits])
    p = functools.reduce(lambda x, y: x + y, [jnp.exp(x - m) for x in logits])
    den = jnp.sum(p, axis=1, keepdims=True)
    pb = p.astype(BF16)
    o = _dot_nt(pb[:, :c_buf], vb_ref[...].reshape(wh, c_buf).astype(BF16))
    o = (o + _dot(pb[:, c_buf:], _pad_rows_bf16(vn_ref[...], LANES))) / den
    _store_head_rows(o_ref, o, n_new)


def _dil_sample_bias(rel_bias, n_new, c_buf, heads, halves):
    t = np.arange(n_new)[:, None]
    idx = np.arange(c_buf + LANES)[None, :]
    hh = heads // halves
    tabs = []
    for window, dil in DIL_PAIRS:
        dist = c_buf + t - idx
        valid = (dist >= 0) & (dist % dil == 0) & (dist // dil <= window // dil) & (idx < c_buf + n_new)
        bucket = _rel_bucket_np(np.clip(dist, 0, window))
        bias = rel_bias[jnp.asarray(bucket)].astype(F32)
        bias = jnp.where(jnp.asarray(valid)[:, :, None], bias, NEG)
        bias = bias.reshape(n_new, -1, halves, hh).transpose(2, 0, 3, 1)
        tabs.append(bias.reshape(halves, n_new * hh, -1))
    return jnp.stack(tabs)


def _dil_sample(q, k_new, v_new, buf_k, buf_v, layer, bias_tab):
    b, t, w = q.shape
    heads, hd, c_buf = buf_k.shape[2:]
    n_br, halves, rows, n_keys = bias_tab.shape
    hh, wh = heads // halves, w // halves
    qbd = _block_diag_queries(q, heads, halves)
    kn, vn = _pad_new_rows(k_new), _pad_new_rows(v_new)
    n_pad = kn.shape[1]
    buf_spec = pl.BlockSpec((None, None, hh, hd, c_buf), lambda b_, hf: (layer, b_, hf, 0, 0))
    new_spec = pl.BlockSpec((None, n_pad, wh), lambda b_, hf: (b_, 0, hf))
    out = pl.pallas_call(
        functools.partial(_dil_sample_kernel, n_new=t),
        grid=(b, halves),
        in_specs=[pl.BlockSpec((None, None, rows, wh), lambda b_, hf: (b_, hf, 0, 0)),
                  buf_spec, buf_spec, new_spec, new_spec,
                  pl.BlockSpec((n_br, None, rows, n_keys), lambda b_, hf: (0, hf, 0, 0))],
        out_specs=pl.BlockSpec((None, t, wh), lambda b_, hf: (b_, 0, hf)),
        out_shape=jax.ShapeDtypeStruct((b, t, w), F32),
        compiler_params=_cparams("parallel", "arbitrary"),
        name="dil_sample",
    )(qbd, buf_k, buf_v, kn, vn, bias_tab)
    return out.reshape(b * t, w)


def _dil_prompt_bias(rel_bias, dil):
    qi = np.arange(DIL_BACK)[:, None]
    kk = np.arange(2 * DIL_BACK)[None, :]
    j = qi + DIL_BACK - kk
    valid = (j >= 0) & (j <= DIL_BACK)
    bucket = _rel_bucket_np(np.clip(j, 0, DIL_BACK) * dil)
    bias = rel_bias[jnp.asarray(bucket)].astype(F32).transpose(2, 0, 1)
    return jnp.where(jnp.asarray(valid)[None], bias, NEG)


def _block_diag_weights(wb):
    n, c, _ = wb.shape
    eye = jnp.asarray(np.eye(n, dtype=np.float32))
    return (wb[:, :, None, :] * eye[:, None, :, None]).reshape(n * c, n * c).astype(BF16)


TM_PROJ = 512
TM_MLP = 512
TF_MLP = 1024
T_SB = 512
TS_LRU = 512
TM_MERGE = 512
SB_PAGE_SLOTS = 8


def kernel(x_prompt, x_sample, cache_sb_k, cache_sb_v, state_conv, state_lru, cache_dil_k, cache_dil_v,
           page_table, rel_bias, norm_mix, norm_ffn, norm_final, w_in_ab, sb_bias, conv_w, conv_b,
           lru_wa, lru_ba, lru_wx, lru_bx, lru_lambda, w_out_ab, w_in_c, w_out_c, w_ff1, w_ff2):
    bp, s, d = x_prompt.shape
    bs, t_new, _ = x_sample.shape
    depth = norm_mix.shape[0]
    sbw = SB_HEADS * HEAD_DIM
    lw = conv_w.shape[2]
    dil_heads = w_in_c.shape[2] // 3 // HEAD_DIM
    dw = dil_heads * HEAD_DIM
    qscale = 1.0 / math.sqrt(HEAD_DIM)
    c_buf = cache_dil_k.shape[2]
    keep = min(DIL_MAX_WINDOW, s)

    tri = _suffix_matrix()
    tok_minor = lambda a: a.transpose(0, 1, 3, 4, 2)
    pool_k, pool_v = tok_minor(cache_sb_k), tok_minor(cache_sb_v)
    buf_k, buf_v = tok_minor(cache_dil_k), tok_minor(cache_dil_v)
    dil_bias_p = [_dil_prompt_bias(rel_bias, dil) for _, dil in DIL_PAIRS]
    dil_bias_s = _dil_sample_bias(rel_bias, t_new, c_buf, dil_heads, 2)
    lane_to_head = np.arange(LANES)[:, None] == (np.arange(dw) // HEAD_DIM)[None, :]
    expand = jnp.asarray(lane_to_head.astype(np.float32), dtype=BF16)

    hp = x_prompt.reshape(bp * s, d)
    hs = x_sample.reshape(bs * t_new, d)
    outs = {name: [] for name in ("sbk_p", "sbv_p", "sbk_s", "sbv_s", "conv_p", "conv_s", "lru_p", "lru_s",
                                  "dk_p", "dv_p", "dk_s", "dv_s")}
    y_prompt = y_sample = None
    for layer in range(depth):
        i = layer // 2
        last = layer == depth - 1
        g_fin = norm_final if last else None
        w1 = w_ff1[layer].astype(BF16)
        w2 = w_ff2[layer].astype(BF16)
        if layer % 2 == 0:
            w_in = w_in_ab[i].astype(BF16)
            w_out = w_out_ab[i].astype(BF16)
            wo_parts = [w_out[:sbw], w_out[sbw:]]
            wa = _block_diag_weights(lru_wa[i])
            wx = _block_diag_weights(lru_wx[i])
            lru_args = (conv_w[i], conv_b[i], wa, lru_ba[i], wx, lru_bx[i], lru_lambda[i])
            bias_rows = jnp.broadcast_to(sb_bias[i].astype(F32)[:, None], (SB_HEADS, LANES))
            spec = [(0, sbw, qscale), (sbw, sbw, 1.0), (sbw, sbw, 1.0), (2 * sbw, sbw, 1.0), (2 * sbw, sbw, 1.0),
                    (3 * sbw, lw, 1.0), (3 * sbw + lw, lw, 1.0)]
            qb, k32, kb, v32, vb, xr, gate = _norm_proj(hp, norm_mix[layer], w_in, spec,
                                                        [BF16, F32, BF16, F32, BF16, F32, F32], TM_PROJ)
            r3 = lambda a: a.reshape(bp, s, a.shape[1])
            attn = _sb_prompt(r3(qb), r3(kb), r3(vb), bias_rows, tri, T_SB)
            lru_o, h_last = _lru_prompt(r3(xr), r3(gate), *lru_args, TS_LRU)
            outs["sbk_p"].append(k32.reshape(bp, s, SB_HEADS, HEAD_DIM))
            outs["sbv_p"].append(v32.reshape(bp, s, SB_HEADS, HEAD_DIM))
            outs["conv_p"].append(r3(xr)[:, s - (CONV_W - 1):])
            outs["lru_p"].append(h_last.reshape(bp, lw))
            hp = _out_mlp(hp, [attn.reshape(bp * s, sbw), lru_o.reshape(bp * s, lw)], wo_parts,
                          norm_ffn[layer], w1, w2, g_fin, TM_MLP, TF_MLP)
            spec_s = [(0, sbw, 1.0), (sbw, sbw, 1.0), (2 * sbw, sbw, 1.0), (3 * sbw, lw, 1.0), (3 * sbw + lw, lw, 1.0)]
            q_s, k_s, v_s, xr_s, g_s = _norm_proj(hs, norm_mix[layer], w_in, spec_s, [F32] * 5, TM_PROJ)
            r3s = lambda a: a.reshape(bs, t_new, a.shape[1])
            attn_s = _sb_sample(r3s(q_s), r3s(k_s), r3s(v_s), pool_k, pool_v, i, page_table, sb_bias[i],
                                tri, SB_PAGE_SLOTS)
            tmaj = lambda a: a.transpose(1, 0, 2)
            xr_s3 = r3s(xr_s)
            lru_os, h_last_s = _lru_sample(tmaj(xr_s3), tmaj(r3s(g_s)), tmaj(state_conv[i]), state_lru[i],
                                           *lru_args)
            outs["sbk_s"].append(k_s.reshape(bs, t_new, SB_HEADS, HEAD_DIM))
            outs["sbv_s"].append(v_s.reshape(bs, t_new, SB_HEADS, HEAD_DIM))
            conv_cat = jnp.concatenate([state_conv[i].astype(F32), xr_s3], axis=1)
            outs["conv_s"].append(conv_cat[:, t_new:])
            outs["lru_s"].append(h_last_s)
            hs = _out_mlp(hs, [attn_s, tmaj(lru_os).reshape(bs * t_new, lw)], wo_parts,
                          norm_ffn[layer], w1, w2, g_fin, TM_MLP, TF_MLP)
        else:
            w_in = w_in_c[i].astype(BF16)
            w_out = w_out_c[i].astype(BF16)
            spec = [(0, dw, qscale), (dw, dw, 1.0), (dw, dw, 1.0), (2 * dw, dw, 1.0), (2 * dw, dw, 1.0)]
            qb, k32, kb, v32, vb = _norm_proj(hp, norm_mix[layer], w_in, spec, [BF16, F32, BF16, F32, BF16], TM_PROJ)
            r3 = lambda a: a.reshape(bp, s, a.shape[1])
            branches = [_dil_prompt_branch(r3(qb), r3(kb), r3(vb), tab, dil)
                        for tab, (_, dil) in zip(dil_bias_p, DIL_PAIRS)]
            o_p = _dil_merge(branches, expand, TM_MERGE)
            outs["dk_p"].append(r3(k32)[:, s - keep:].reshape(bp, keep, dil_heads, HEAD_DIM))
            outs["dv_p"].append(r3(v32)[:, s - keep:].reshape(bp, keep, dil_heads, HEAD_DIM))
            hp = _out_mlp(hp, [o_p], [w_out], norm_ffn[layer], w1, w2, g_fin, TM_MLP, TF_MLP)
            spec_s = [(0, dw, 1.0), (dw, dw, 1.0), (2 * dw, dw, 1.0)]
            q_s, k_s, v_s = _norm_proj(hs, norm_mix[layer], w_in, spec_s, [F32] * 3, TM_PROJ)
            r3s = lambda a: a.reshape(bs, t_new, a.shape[1])
            o_s = _dil_sample(r3s(q_s), r3s(k_s), r3s(v_s), buf_k, buf_v, i, dil_bias_s)
            outs["dk_s"].append(k_s.reshape(bs, t_new, dil_heads, HEAD_DIM))
            outs["dv_s"].append(v_s.reshape(bs, t_new, dil_heads, HEAD_DIM))
            hs = _out_mlp(hs, [o_s], [w_out], norm_ffn[layer], w1, w2, g_fin, TM_MLP, TF_MLP)
    y_prompt = hp.reshape(bp, s, d)
    y_sample = hs.reshape(bs, t_new, d)
    st = lambda name: jnp.stack(outs[name])
    return (y_prompt, y_sample, st("sbk_p"), st("sbv_p"), st("sbk_s"), st("sbv_s"),
            st("conv_p"), st("conv_s"), st("lru_p"), st("lru_s"),
            st("dk_p"), st("dv_p"), st("dk_s"), st("dv_s"))
```

```python
import functools
import math

import numpy as np
import jax
import jax.numpy as jnp
from jax import lax
from jax.experimental import pallas as pl
from jax.experimental.pallas import tpu as pltpu

F32 = jnp.float32
BF16 = jnp.bfloat16

HEAD_DIM = 64
SB_HEADS = 8
LRU_BLOCKS = 8
CONV_W = 4
RGLRU_C = 8.0
DIL_PAIRS = ((128, 1), (512, 4), (2048, 16))
DIL_MAX_WINDOW = 2048
DIL_BACK = 128
REL_BUCKETS = 32
REL_MAX_DIST = 2048
NORM_EPS = 1e-6
NEG = -1e30
LOG2E = 1.4426950408889634
LANES = 128
HEAD_GROUP = 4
GROUP_W = HEAD_GROUP * HEAD_DIM
VMEM_LIMIT = 56 * 1024 * 1024


def _cparams(*sem):
    return pltpu.CompilerParams(dimension_semantics=sem, vmem_limit_bytes=VMEM_LIMIT)


def _softplus(z):
    neg_abs = lax.bitcast_convert_type(lax.bitcast_convert_type(z, jnp.uint32) | jnp.uint32(0x80000000), F32)
    return jnp.maximum(z, 0.0) + jnp.log(1.0 + jnp.exp2(neg_abs * LOG2E))


def _sigmoid(z):
    return 1.0 / (1.0 + jnp.exp(-z))


def _gelu_tanh(x):
    c = math.sqrt(2.0 / math.pi)
    return 0.5 * x * (1.0 + jnp.tanh(c * (x + 0.044715 * (x * x * x))))


def _rms(x, g):
    inv = lax.rsqrt(jnp.mean(x * x, axis=-1, keepdims=True) + NORM_EPS)
    return x * inv * g


def _dot(a, b):
    return jnp.dot(a, b, preferred_element_type=F32)


def _dot_nt(a, b):
    return lax.dot_general(a, b, (((1,), (1,)), ((), ())), preferred_element_type=F32)


def _pad_rows_bf16(x, rows):
    pad = jnp.zeros((rows - x.shape[0], x.shape[1]), F32)
    return jnp.concatenate([x, pad], axis=0).astype(BF16)


def _split_hi_lo(x):
    hi = lax.bitcast_convert_type(lax.bitcast_convert_type(x, jnp.uint32) & jnp.uint32(0xFFFF0000), F32)
    return jnp.concatenate([hi.astype(BF16), (x - hi).astype(BF16)], axis=1)


def _suffix_matrix():
    j = np.arange(LANES)[:, None]
    s = np.arange(LANES)[None, :]
    half = np.concatenate([(j >= s).astype(np.float32), np.ones((LANES, LANES), np.float32)], axis=1)
    return jnp.asarray(np.concatenate([half, half], axis=0), dtype=BF16)


def _norm_proj_kernel(x_ref, g_ref, w_ref, *out_refs, outs):
    xn = _rms(x_ref[...], g_ref[...]).astype(BF16)
    cache = {}
    for o_ref, (off, width, scale) in zip(out_refs, outs):
        if (off, width) not in cache:
            cache[off, width] = _dot(xn, w_ref[:, off:off + width])
        z = cache[off, width]
        if scale != 1.0:
            z = z * scale
        o_ref[...] = z.astype(o_ref.dtype)


def _norm_proj(x, g, w, outs, dtypes, tm):
    m, d = x.shape
    n = w.shape[1]
    tm = min(tm, m)
    kern = functools.partial(_norm_proj_kernel, outs=tuple(outs))
    return pl.pallas_call(
        kern,
        grid=(m // tm,),
        in_specs=[pl.BlockSpec((tm, d), lambda i: (i, 0)),
                  pl.BlockSpec((1, d), lambda i: (0, 0)),
                  pl.BlockSpec((d, n), lambda i: (0, 0))],
        out_specs=[pl.BlockSpec((tm, wd), lambda i: (i, 0)) for (_, wd, _) in outs],
        out_shape=[jax.ShapeDtypeStruct((m, wd), dt) for (_, wd, _), dt in zip(outs, dtypes)],
        compiler_params=_cparams("parallel"),
        name="norm_proj",
    )(x, g.reshape(1, d), w)


def _dil_proj_kernel(x_ref, g_ref, w_ref, *refs, dils, qscale):
    n = len(dils)
    q_refs, k_refs, v_refs = refs[:n], refs[n:2 * n], refs[2 * n:3 * n]
    k32_ref, v32_ref, zs_ref = refs[3 * n:]
    tm = x_ref.shape[0]
    wd = w_ref.shape[1] // 3
    xn = _rms(x_ref[...], g_ref[...]).astype(BF16)
    for idx, (o_refs, f32_ref, scale) in enumerate(((q_refs, None, qscale), (k_refs, k32_ref, 1.0),
                                                    (v_refs, v32_ref, 1.0))):
        z = _dot(xn, w_ref[:, idx * wd:(idx + 1) * wd])
        if f32_ref is not None:
            f32_ref[...] = z
        for c in range(wd // LANES):
            zs_ref[c] = z[:, c * LANES:(c + 1) * LANES] * scale
        for o_ref, d in zip(o_refs, dils):
            for r in range(d):
                for c in range(wd // LANES):
                    o_ref[r, :, c * LANES:(c + 1) * LANES] = (
                        zs_ref[c, pl.ds(r, tm // d, stride=d), :].astype(BF16))


def _dil_proj(x, g, w, dils, tm):
    b, s, d_model = x.shape
    wd = w.shape[1] // 3
    tm = min(tm, s)
    n = len(dils)
    res_spec = lambda d: pl.BlockSpec((None, d, tm // d, wd), lambda b_, i: (b_, 0, i, 0))
    res_shape = lambda d: jax.ShapeDtypeStruct((b, d, s // d, wd), BF16)
    tok_spec = pl.BlockSpec((None, tm, wd), lambda b_, i: (b_, i, 0))
    outs = pl.pallas_call(
        functools.partial(_dil_proj_kernel, dils=tuple(dils), qscale=1.0 / math.sqrt(HEAD_DIM)),
        grid=(b, s // tm),
        in_specs=[pl.BlockSpec((None, tm, d_model), lambda b_, i: (b_, i, 0)),
                  pl.BlockSpec((1, d_model), lambda b_, i: (0, 0)),
                  pl.BlockSpec(w.shape, lambda b_, i: (0, 0))],
        out_specs=[res_spec(d) for d in dils] * 3 + [tok_spec, tok_spec],
        out_shape=[res_shape(d) for d in dils] * 3 + [jax.ShapeDtypeStruct((b, s, wd), F32)] * 2,
        scratch_shapes=[pltpu.VMEM((wd // LANES, tm, LANES), F32)],
        compiler_params=_cparams("parallel", "parallel"),
        name="dil_proj",
    )(x, g.reshape(1, d_model), w)
    return outs[:n], outs[n:2 * n], outs[2 * n:3 * n], outs[3 * n], outs[3 * n + 1]


def _out_mlp_kernel(*refs, n_parts, final):
    h_ref = refs[0]
    part_refs = refs[1:1 + n_parts]
    wo_refs = refs[1 + n_parts:1 + 2 * n_parts]
    rest = refs[1 + 2 * n_parts:]
    if final:
        gf_ref, w1_ref, w2_ref, gfin_ref, o_ref, hn_ref, xn_ref, acc_ref = rest
    else:
        gf_ref, w1_ref, w2_ref, o_ref, hn_ref, xn_ref, acc_ref = rest
    f = pl.program_id(1)

    @pl.when(f == 0)
    def _():
        hn = h_ref[...]
        for p_ref, wo_ref in zip(part_refs, wo_refs):
            hn = hn + _dot(p_ref[...].astype(BF16), wo_ref[...])
        hn_ref[...] = hn
        xn_ref[...] = _rms(hn, gf_ref[...]).astype(BF16)
        acc_ref[...] = jnp.zeros_like(acc_ref)

    u = jnp.maximum(_dot(xn_ref[...], w1_ref[...]), 0.0)
    acc_ref[...] += _dot((u * u).astype(BF16), w2_ref[...])

    @pl.when(f == pl.num_programs(1) - 1)
    def _():
        out = hn_ref[...] + acc_ref[...]
        if final:
            out = _rms(out, gfin_ref[...])
        o_ref[...] = out


def _out_mlp(h, parts, wouts, g_ffn, w1, w2, g_final, tm, tf):
    m, d = h.shape
    dff = w1.shape[1]
    tm = min(tm, m)
    final = g_final is not None
    n_parts = len(parts)
    in_specs = [pl.BlockSpec((tm, d), lambda i, f: (i, 0))]
    in_specs += [pl.BlockSpec((tm, p.shape[1]), lambda i, f: (i, 0)) for p in parts]
    in_specs += [pl.BlockSpec(wo.shape, lambda i, f: (0, 0)) for wo in wouts]
    in_specs += [pl.BlockSpec((1, d), lambda i, f: (0, 0)),
                 pl.BlockSpec((d, tf), lambda i, f: (0, f)),
                 pl.BlockSpec((tf, d), lambda i, f: (f, 0))]
    args = [h, *parts, *wouts, g_ffn.reshape(1, d), w1, w2]
    if final:
        in_specs.append(pl.BlockSpec((1, d), lambda i, f: (0, 0)))
        args.append(g_final.reshape(1, d))
    kern = functools.partial(_out_mlp_kernel, n_parts=n_parts, final=final)
    return pl.pallas_call(
        kern,
        grid=(m // tm, dff // tf),
        in_specs=in_specs,
        out_specs=pl.BlockSpec((tm, d), lambda i, f: (i, 0)),
        out_shape=jax.ShapeDtypeStruct((m, d), F32),
        scratch_shapes=[pltpu.VMEM((tm, d), F32), pltpu.VMEM((tm, d), BF16), pltpu.VMEM((tm, d), F32)],
        compiler_params=_cparams("parallel", "arbitrary"),
        name="out_mlp",
    )(*args)


def _head_lane_mask(shape, h):
    lane = lax.broadcasted_iota(jnp.int32, shape, len(shape) - 1)
    return (lane >= h * HEAD_DIM) & (lane < (h + 1) * HEAD_DIM)


def _sb_block(z, r_prev, tri):
    rr = _dot(_split_hi_lo(_softplus(z)), tri)
    w = jnp.exp(z - rr[:, :LANES] - r_prev)
    return w, r_prev + rr[:, LANES:]


def _sb_prompt_kernel(it_ref, jt_ref, q_ref, k_ref, v_ref, bias_ref, tri_ref, o_ref,
                      qm_ref, r_ref, acc_ref, *, t):
    g = pl.program_id(1)
    p = pl.program_id(2)
    i = it_ref[p]
    j = jt_ref[p]

    @pl.when(j == i)
    def _():
        q = q_ref[...]
        for h in range(HEAD_GROUP):
            qm_ref[h] = jnp.where(_head_lane_mask(q.shape, h), q, jnp.zeros_like(q))
        r_ref[...] = jnp.zeros_like(r_ref)
        acc_ref[...] = jnp.zeros_like(acc_ref)

    def run(diag):
        span = 2 * LANES
        for h in range(HEAD_GROUP):
            qh = qm_ref[h]
            bias = bias_ref[pl.ds(g * HEAD_GROUP + h, 1), :]
            r = r_ref[h]
            acc = acc_ref[h]
            for half in reversed(range(t // span)):
                keys = slice(half * span, (half + 1) * span)
                s = _dot_nt(qh, k_ref[keys, :])
                ws = [None] * (span // LANES)
                for c in reversed(range(span // LANES)):
                    z = s[:, c * LANES:(c + 1) * LANES] + bias
                    if diag:
                        row = lax.broadcasted_iota(jnp.int32, z.shape, 0)
                        col = lax.broadcasted_iota(jnp.int32, z.shape, 1) + (half * span + c * LANES)
                        z = jnp.where(col < row, z, NEG)
                    w, r = _sb_block(z, r, tri_ref[...])
                    ws[c] = w.astype(BF16)
                acc = acc + _dot(jnp.concatenate(ws, axis=1), v_ref[keys, :])
            r_ref[h] = r
            acc_ref[h] = acc

    @pl.when(j == i)
    def _():
        run(True)

    @pl.when(j != i)
    def _():
        run(False)

    @pl.when(j == 0)
    def _():
        out = jnp.zeros(o_ref.shape, F32)
        for h in range(HEAD_GROUP):
            a = acc_ref[h]
            out = jnp.where(_head_lane_mask(a.shape, h), a, out)
        o_ref[...] = out.astype(o_ref.dtype)


def _sb_prompt(q, k, v, bias_rows, tri, t):
    b, s, w = q.shape
    t = min(t, s)
    nq = s // t
    it = np.concatenate([np.full(i + 1, i) for i in range(nq)]).astype(np.int32)
    jt = np.concatenate([np.arange(i, -1, -1) for i in range(nq)]).astype(np.int32)
    grid_spec = pltpu.PrefetchScalarGridSpec(
        num_scalar_prefetch=2,
        grid=(b, w // GROUP_W, len(it)),
        in_specs=[pl.BlockSpec((None, t, GROUP_W), lambda b_, g, p, it_, jt_: (b_, it_[p], g)),
                  pl.BlockSpec((None, t, GROUP_W), lambda b_, g, p, it_, jt_: (b_, jt_[p], g)),
                  pl.BlockSpec((None, t, GROUP_W), lambda b_, g, p, it_, jt_: (b_, jt_[p], g)),
                  pl.BlockSpec(bias_rows.shape, lambda b_, g, p, it_, jt_: (0, 0)),
                  pl.BlockSpec(tri.shape, lambda b_, g, p, it_, jt_: (0, 0))],
        out_specs=pl.BlockSpec((None, t, GROUP_W), lambda b_, g, p, it_, jt_: (b_, it_[p], g)),
        scratch_shapes=[pltpu.VMEM((HEAD_GROUP, t, GROUP_W), BF16),
                        pltpu.VMEM((HEAD_GROUP, t, LANES), F32),
                        pltpu.VMEM((HEAD_GROUP, t, GROUP_W), F32)])
    return pl.pallas_call(
        functools.partial(_sb_prompt_kernel, t=t),
        grid_spec=grid_spec,
        out_shape=jax.ShapeDtypeStruct((b, s, w), BF16),
        compiler_params=_cparams("parallel", "parallel", "arbitrary"),
        name="sb_prompt",
    )(jnp.asarray(it), jnp.asarray(jt), q, k, v, bias_rows, tri)


def _block_diag_queries(q, heads, halves):
    b, t, w = q.shape
    hh = heads // halves
    wh = w // halves
    qs = (q * (1.0 / math.sqrt(HEAD_DIM))).reshape(b, t, halves, 1, wh)
    lane_head = (np.arange(wh) // HEAD_DIM)[None, :]
    mask = jnp.asarray(lane_head == np.arange(hh)[:, None])
    qbd = jnp.where(mask[None, None, None], qs, 0.0)
    return qbd.transpose(0, 2, 1, 3, 4).reshape(b, halves, t * hh, wh).astype(BF16)


def _pad_new_rows(x):
    t = x.shape[1]
    return jnp.pad(x, ((0, 0), (0, -t % 8), (0, 0)))


def _store_head_rows(o_ref, x, n_new):
    heads = x.shape[0] // n_new
    shape = (heads, x.shape[1])
    lane_head = jnp.right_shift(lax.broadcasted_iota(jnp.int32, shape, 1), HEAD_DIM.bit_length() - 1)
    sel = lane_head == lax.broadcasted_iota(jnp.int32, shape, 0)
    for tk in range(n_new):
        row = jnp.sum(jnp.where(sel, x[tk * heads:(tk + 1) * heads], 0.0), axis=0, keepdims=True)
        o_ref[tk:tk + 1, :] = row.astype(o_ref.dtype)


def _sb_sample_kernel(*refs, n_slots, n_new):
    pt_ref, qbd_ref, kn_ref, vn_ref, bias_ref, tri_ref = refs[:6]
    kp_refs = refs[6:6 + n_slots]
    vp_refs = refs[6 + n_slots:6 + 2 * n_slots]
    o_ref, r_ref, acc_ref = refs[6 + 2 * n_slots:]
    del pt_ref
    c = pl.program_id(1)
    rows, w = qbd_ref.shape
    heads = rows // n_new
    qbd = qbd_ref[...]
    bias = bias_ref[...]
    tri = tri_ref[...]

    @pl.when(c == 0)
    def _():
        z = _dot_nt(qbd, _pad_rows_bf16(kn_ref[...], LANES)) + bias
        tok = jnp.right_shift(lax.broadcasted_iota(jnp.int32, z.shape, 0), heads.bit_length() - 1)
        z = jnp.where(lax.broadcasted_iota(jnp.int32, z.shape, 1) < tok, z, NEG)
        wgt, r = _sb_block(z, jnp.zeros((rows, LANES), F32), tri)
        r_ref[...] = r
        acc_ref[...] = _dot(wgt.astype(BF16), _pad_rows_bf16(vn_ref[...], LANES))

    cat = lambda page_refs: jnp.concatenate([p[...].reshape(w, LANES).astype(BF16) for p in page_refs], axis=1)
    z = _dot(qbd, cat(kp_refs))
    zs = [z[:, p * LANES:(p + 1) * LANES] + bias for p in range(n_slots)]
    rr = _dot(jnp.concatenate([_split_hi_lo(_softplus(zp)) for zp in zs], axis=0), tri)
    r = r_ref[...]
    ws = []
    for p, zp in enumerate(zs):
        rp = rr[p * rows:(p + 1) * rows]
        ws.append(jnp.exp(zp - rp[:, :LANES] - r).astype(BF16))
        r = r + rp[:, LANES:]
    r_ref[...] = r
    acc = acc_ref[...] + _dot_nt(jnp.concatenate(ws, axis=1), cat(vp_refs))
    acc_ref[...] = acc

    @pl.when(c == pl.num_programs(1) - 1)
    def _():
        _store_head_rows(o_ref, acc, n_new)


def _sb_sample(q, k_new, v_new, pool_k, pool_v, layer, page_table, sb_bias, tri, n_slots):
    b, t, w = q.shape
    heads, hd, page = pool_k.shape[2:]
    n_pages = page_table.shape[1]
    assert page == LANES and n_pages % n_slots == 0 and heads & (heads - 1) == 0
    n_chunks = n_pages // n_slots
    rows = t * heads
    qbd = _block_diag_queries(q, heads, 1).reshape(b, rows, w)
    kn, vn = _pad_new_rows(k_new), _pad_new_rows(v_new)
    n_pad = kn.shape[1]
    bias_col = jnp.broadcast_to(jnp.tile(sb_bias.astype(F32), t)[:, None], (rows, LANES))

    def pool_spec(slot):
        return pl.BlockSpec(
            (None, None, heads, hd, page),
            lambda b_, c, pt: (layer, pt[b_, n_pages - 1 - (c * n_slots + slot)], 0, 0, 0))

    new_spec = pl.BlockSpec((None, n_pad, w), lambda b_, c, pt: (b_, 0, 0))
    grid_spec = pltpu.PrefetchScalarGridSpec(
        num_scalar_prefetch=1,
        grid=(b, n_chunks),
        in_specs=[pl.BlockSpec((None, rows, w), lambda b_, c, pt: (b_, 0, 0)), new_spec, new_spec,
                  pl.BlockSpec(bias_col.shape, lambda b_, c, pt: (0, 0)),
                  pl.BlockSpec(tri.shape, lambda b_, c, pt: (0, 0))]
                 + [pool_spec(sl) for sl in range(n_slots)] * 2,
        out_specs=pl.BlockSpec((None, t, w), lambda b_, c, pt: (b_, 0, 0)),
        scratch_shapes=[pltpu.VMEM((rows, LANES), F32), pltpu.VMEM((rows, w), F32)])
    out = pl.pallas_call(
        functools.partial(_sb_sample_kernel, n_slots=n_slots, n_new=t),
        grid_spec=grid_spec,
        out_shape=jax.ShapeDtypeStruct((b, t, w), F32),
        compiler_params=_cparams("parallel", "arbitrary"),
        name="sb_sample",
    )(page_table, qbd, kn, vn, bias_col, tri, *([pool_k] * n_slots), *([pool_v] * n_slots))
    return out.reshape(b * t, w)


def _lru_gates(xc, wa_ref, ba_ref, wx_ref, bx_ref, lam_ref):
    xcb = xc.astype(BF16)
    r = _sigmoid(_dot(xcb, wa_ref[...]) + ba_ref[...])
    ig = _sigmoid(_dot(xcb, wx_ref[...]) + bx_ref[...])
    log_a = (-RGLRU_C * _softplus(-lam_ref[...])) * r
    a = jnp.exp(log_a)
    inp = jnp.sqrt(-jnp.tanh(log_a) * (a * a + 1.0)) * (ig * xc)
    return a, inp


def _lru_prompt_kernel(xr_ref, g_ref, cw_ref, cb_ref, wa_ref, ba_ref, wx_ref, bx_ref, lam_ref,
                       o_ref, hl_ref, xp_ref, h_ref, *, ts):
    c = pl.program_id(1)
    pad = xp_ref.shape[0] - ts

    @pl.when(c == 0)
    def _():
        xp_ref[0:pad, :] = jnp.zeros((pad, xp_ref.shape[1]), F32)
        h_ref[...] = jnp.zeros_like(h_ref)

    x = xr_ref[...]
    xp_ref[pad:pad + ts, :] = x
    xc = cb_ref[...]
    for jj in range(CONV_W):
        sh = CONV_W - 1 - jj
        xc = xc + cw_ref[jj:jj + 1, :] * xp_ref[pad - sh:pad - sh + ts, :]
    xp_ref[0:pad, :] = x[ts - pad:ts, :]

    a, bt = _lru_gates(xc, wa_ref, ba_ref, wx_ref, bx_ref, lam_ref)
    row = lax.broadcasted_iota(jnp.int32, a.shape, 0)
    d = 1
    while d < ts:
        keep = row >= d
        bt = jnp.where(keep, a * pltpu.roll(bt, d, 0) + bt, bt)
        a = jnp.where(keep, a * pltpu.roll(a, d, 0), a)
        d *= 2
    hs = a * h_ref[...] + bt
    h_ref[...] = hs[ts - 1:ts, :]
    hl_ref[...] = hs[ts - 1:ts, :]
    o_ref[...] = (_gelu_tanh(g_ref[...]) * hs).astype(o_ref.dtype)


def _lru_prompt(xr, g, cw, cb, wa, ba, wx, bx, lam, ts):
    b, s, w = xr.shape
    ts = min(ts, s)
    row = lambda a: a.reshape(1, w)
    vec = pl.BlockSpec((1, w), lambda b_, c: (0, 0))
    mat = pl.BlockSpec((w, w), lambda b_, c: (0, 0))
    seq = pl.BlockSpec((None, ts, w), lambda b_, c: (b_, c, 0))
    return pl.pallas_call(
        functools.partial(_lru_prompt_kernel, ts=ts),
        grid=(b, s // ts),
        in_specs=[seq, seq, pl.BlockSpec((CONV_W, w), lambda b_, c: (0, 0)), vec, mat, vec, mat, vec, vec],
        out_specs=[seq, pl.BlockSpec((None, 1, w), lambda b_, c: (b_, 0, 0))],
        out_shape=[jax.ShapeDtypeStruct((b, s, w), BF16), jax.ShapeDtypeStruct((b, 1, w), F32)],
        scratch_shapes=[pltpu.VMEM((ts + 8, w), F32), pltpu.VMEM((1, w), F32)],
        compiler_params=_cparams("parallel", "arbitrary"),
        name="lru_prompt",
    )(xr, g, cw, row(cb), wa, row(ba), wx, row(bx), row(lam))


def _lru_sample_kernel(xr_ref, g_ref, st_ref, h0_ref, cw_ref, cb_ref, wa_ref, ba_ref, wx_ref, bx_ref,
                       lam_ref, o_ref, hl_ref):
    n_t = xr_ref.shape[0]
    n_s = st_ref.shape[0]
    xs = [st_ref[i] for i in range(n_s)] + [xr_ref[i] for i in range(n_t)]
    h = h0_ref[...]
    for tt in range(n_t):
        xc = cb_ref[...]
        for jj in range(CONV_W):
            xc = xc + cw_ref[jj:jj + 1, :] * xs[tt + jj]
        a, inp = _lru_gates(xc, wa_ref, ba_ref, wx_ref, bx_ref, lam_ref)
        h = a * h + inp
        o_ref[tt] = (_gelu_tanh(g_ref[tt]) * h).astype(o_ref.dtype)
    hl_ref[...] = h


def _lru_sample(xr_t, g_t, state_t, h0, cw, cb, wa, ba, wx, bx, lam):
    n_t, b, w = xr_t.shape
    row = lambda a: a.reshape(1, w)
    return pl.pallas_call(
        _lru_sample_kernel,
        out_shape=[jax.ShapeDtypeStruct((n_t, b, w), BF16), jax.ShapeDtypeStruct((b, w), F32)],
        compiler_params=pltpu.CompilerParams(vmem_limit_bytes=VMEM_LIMIT),
        name="lru_sample",
    )(xr_t, g_t, state_t, h0, cw, row(cb), wa, row(ba), wx, row(bx), row(lam))


def _rel_bucket_np(dist):
    exact = REL_BUCKETS // 2
    df = np.maximum(dist.astype(np.float64), 1.0)
    large = exact + (np.log(df / exact) / math.log(REL_MAX_DIST / exact) * (REL_BUCKETS - exact)).astype(np.int64)
    large = np.minimum(large, REL_BUCKETS - 1)
    return np.where(dist < exact, dist, large)


def _dil_prompt_kernel(q_ref, kp_ref, kc_ref, vp_ref, vc_ref, bias_ref, o_ref, m_ref, l_ref, *, dil):
    tq = q_ref.shape[0]
    n_groups = q_ref.shape[1] // GROUP_W
    res = pl.program_id(2)
    kcol = lax.broadcasted_iota(jnp.int32, (HEAD_GROUP * tq, 2 * tq), 1)
    prev_pen = jnp.where(kcol < tq, jnp.where(pl.program_id(1) == 0, NEG, 0.0), 0.0)
    lane = lax.broadcasted_iota(jnp.int32, (tq, LANES), 1)
    m_all = jnp.zeros((tq, LANES), F32)
    l_all = jnp.ones((tq, LANES), F32)
    rows = slice(None) if dil == 1 else pl.ds(res, tq, stride=dil)
    for g in range(n_groups):
        cols = slice(g * GROUP_W, (g + 1) * GROUP_W)
        q = q_ref[:, cols]
        kk = jnp.concatenate([kp_ref[:, cols], kc_ref[:, cols]], axis=0)
        vv = jnp.concatenate([vp_ref[:, cols], vc_ref[:, cols]], axis=0)
        masks = [_head_lane_mask(q.shape, h) for h in range(HEAD_GROUP)]
        qst = jnp.concatenate([jnp.where(hm, q, jnp.zeros_like(q)) for hm in masks], axis=0)
        s = _dot_nt(qst, kk) + bias_ref[g] + prev_pen
        m = jnp.max(s, axis=1, keepdims=True)
        p = jnp.exp(s - m)
        l = jnp.sum(p, axis=1, keepdims=True)
        ost = _dot(p.astype(BF16), vv)
        out = jnp.zeros((tq, GROUP_W), F32)
        for h in range(HEAD_GROUP):
            part = slice(h * tq, (h + 1) * tq)
            out = jnp.where(masks[h], ost[part], out)
            hh = g * HEAD_GROUP + h
            m_all = jnp.where(lane == hh, m[part], m_all)
            l_all = jnp.where(lane == hh, l[part], l_all)
        for j in range(GROUP_W // LANES):
            o_ref[g * (GROUP_W // LANES) + j, rows, :] = out[:, j * LANES:(j + 1) * LANES]
    m_ref[rows, :] = m_all
    l_ref[rows, :] = l_all


def _dil_prompt_branch(q, k, v, bias_tab, dil):
    b, _, length, w = q.shape
    s = length * dil
    tq = DIL_BACK
    n_blk = length // tq
    cur = pl.BlockSpec((None, None, tq, w), lambda b_, i, r: (b_, r, i, 0))
    prev = pl.BlockSpec((None, None, tq, w), lambda b_, i, r: (b_, r, jnp.maximum(i - 1, 0), 0))
    stat = pl.BlockSpec((None, tq * dil, LANES), lambda b_, i, r: (b_, i, 0))
    return pl.pallas_call(
        functools.partial(_dil_prompt_kernel, dil=dil),
        grid=(b, n_blk, dil),
        in_specs=[cur, prev, cur, prev, cur, pl.BlockSpec(bias_tab.shape, lambda b_, i, r: (0, 0, 0))],
        out_specs=[pl.BlockSpec((None, w // LANES, tq * dil, LANES), lambda b_, i, r: (b_, 0, i, 0)), stat, stat],
        out_shape=[jax.ShapeDtypeStruct((b, w // LANES, s, LANES), F32),
                   jax.ShapeDtypeStruct((b, s, LANES), F32),
                   jax.ShapeDtypeStruct((b, s, LANES), F32)],
        compiler_params=_cparams("parallel", "parallel", "arbitrary"),
        name="dil_prompt_d%d" % dil,
    )(q, k, k, v, v, bias_tab)


def _expand_heads(x, e_ref):
    hi = x.astype(BF16)
    r1 = x - hi.astype(F32)
    mid = r1.astype(BF16)
    lo = (r1 - mid.astype(F32)).astype(BF16)
    e = e_ref[...]
    return _dot(hi, e) + _dot(mid, e) + _dot(lo, e)


def _dil_merge_kernel(*refs, n_br):
    o_refs = refs[:n_br]
    m_refs = refs[n_br:2 * n_br]
    l_refs = refs[2 * n_br:3 * n_br]
    e_ref, out_ref = refs[3 * n_br:]
    ms = [r[...] for r in m_refs]
    m_max = functools.reduce(jnp.maximum, ms)
    ws = [jnp.exp(m - m_max) for m in ms]
    tot = functools.reduce(lambda x, y: x + y, [w * r[...] for w, r in zip(ws, l_refs)])
    inv = 1.0 / tot
    coefs = [_expand_heads(w * inv, e_ref) for w in ws]
    for c in range(o_refs[0].shape[0]):
        cols = slice(c * LANES, (c + 1) * LANES)
        acc = functools.reduce(lambda x, y: x + y, [cf[:, cols] * o_ref[c] for cf, o_ref in zip(coefs, o_refs)])
        out_ref[:, cols] = acc.astype(out_ref.dtype)


def _dil_merge(outs, expand, tm):
    n_br = len(outs)
    b, n_cb, s, _ = outs[0][0].shape
    w = n_cb * LANES
    tm = min(tm, s)
    n_t = s // tm
    wide = pl.BlockSpec((None, n_cb, tm, LANES), lambda b_, i: (b_, 0, i, 0))
    thin = pl.BlockSpec((None, tm, LANES), lambda b_, i: (b_, i, 0))
    return pl.pallas_call(
        functools.partial(_dil_merge_kernel, n_br=n_br),
        grid=(b, n_t),
        in_specs=[wide] * n_br + [thin] * (2 * n_br) + [pl.BlockSpec(expand.shape, lambda b_, i: (0, 0))],
        out_specs=pl.BlockSpec((tm, w), lambda b_, i: (b_ * n_t + i, 0)),
        out_shape=jax.ShapeDtypeStruct((b * s, w), BF16),
        compiler_params=_cparams("parallel", "parallel"),
        name="dil_merge",
    )(*[o for o, _, _ in outs], *[m for _, m, _ in outs], *[l for _, _, l in outs], expand)


def _dil_sample_kernel(qbd_ref, kb_ref, vb_ref, kn_ref, vn_ref, bias_ref, o_ref, *, n_new):
    wh, c_buf = kb_ref.shape[0] * kb_ref.shape[1], kb_ref.shape[2]
    qbd = qbd_ref[...]
    s = _dot(qbd, kb_ref[...].reshape(wh, c_buf).astype(BF16))
    sn = _dot_nt(qbd, _pad_rows_bf16(kn_ref[...], LANES))
    z = jnp.concatenate([s, sn], axis=1)
    logits = [z + bias_ref[i] for i in range(bias_ref.shape[0])]
    m = functools.reduce(jnp.maximum, [jnp.max(x, axis=1, keepdims=True) for x in logits])
    p = functools.reduce(lambda x, y: x + y, [jnp.exp(x - m) for x in logits])
    den = jnp.sum(p, axis=1, keepdims=True)
    pb = p.astype(BF16)
    o = _dot_nt(pb[:, :c_buf], vb_ref[...].reshape(wh, c_buf).astype(BF16))
    o = (o + _dot(pb[:, c_buf:], _pad_rows_bf16(vn_ref[...], LANES))) / den
    _store_head_rows(o_ref, o, n_new)


def _dil_sample_bias(rel_bias, n_new, c_buf, heads, halves):
    t = np.arange(n_new)[:, None]
    idx = np.arange(c_buf + LANES)[None, :]
    hh = heads // halves
    tabs = []
    for window, dil in DIL_PAIRS:
        dist = c_buf + t - idx
        valid = (dist >= 0) & (dist % dil == 0) & (dist // dil <= window // dil) & (idx < c_buf + n_new)
        bucket = _rel_bucket_np(np.clip(dist, 0, window))
        bias = rel_bias[jnp.asarray(bucket)].astype(F32)
        bias = jnp.where(jnp.asarray(valid)[:, :, None], bias, NEG)
        bias = bias.reshape(n_new, -1, halves, hh).transpose(2, 0, 3, 1)
        tabs.append(bias.reshape(halves, n_new * hh, -1))
    return jnp.stack(tabs)


def _dil_sample(q, k_new, v_new, buf_k, buf_v, layer, bias_tab):
    b, t, w = q.shape
    heads, hd, c_buf = buf_k.shape[2:]
    n_br, halves, rows, n_keys = bias_tab.shape
    hh, wh = heads // halves, w // halves
    qbd = _block_diag_queries(q, heads, halves)
    kn, vn = _pad_new_rows(k_new), _pad_new_rows(v_new)
    n_pad = kn.shape[1]
    buf_spec = pl.BlockSpec((None, None, hh, hd, c_buf), lambda b_, hf: (layer, b_, hf, 0, 0))
    new_spec = pl.BlockSpec((None, n_pad, wh), lambda b_, hf: (b_, 0, hf))
    out = pl.pallas_call(
        functools.partial(_dil_sample_kernel, n_new=t),
        grid=(b, halves),
        in_specs=[pl.BlockSpec((None, None, rows, wh), lambda b_, hf: (b_, hf, 0, 0)),
                  buf_spec, buf_spec, new_spec, new_spec,
                  pl.BlockSpec((n_br, None, rows, n_keys), lambda b_, hf: (0, hf, 0, 0))],
        out_specs=pl.BlockSpec((None, t, wh), lambda b_, hf: (b_, 0, hf)),
        out_shape=jax.ShapeDtypeStruct((b, t, w), F32),
        compiler_params=_cparams("parallel", "arbitrary"),
        name="dil_sample",
    )(qbd, buf_k, buf_v, kn, vn, bias_tab)
    return out.reshape(b * t, w)


def _dil_prompt_bias(rel_bias, dil):
    qi = np.arange(DIL_BACK)[:, None]
    kk = np.arange(2 * DIL_BACK)[None, :]
    j = qi + DIL_BACK - kk
    valid = (j >= 0) & (j <= DIL_BACK)
    bucket = _rel_bucket_np(np.clip(j, 0, DIL_BACK) * dil)
    onehot = jnp.asarray((bucket[..., None] == np.arange(REL_BUCKETS)).astype(np.float32))
    bias = jnp.einsum("qkb,bh->hqk", onehot, rel_bias.astype(F32), precision=lax.Precision.HIGHEST)
    bias = jnp.where(jnp.asarray(valid)[None], bias, NEG)
    return bias.reshape(bias.shape[0] // HEAD_GROUP, HEAD_GROUP * DIL_BACK, 2 * DIL_BACK)


def _block_diag_weights(wb):
    n, c, _ = wb.shape
    eye = jnp.asarray(np.eye(n, dtype=np.float32))
    return (wb[:, :, None, :] * eye[:, None, :, None]).reshape(n * c, n * c).astype(BF16)


TM_PROJ = 512
TM_MLP = 512
TF_MLP = 1024
T_SB = 512
TS_LRU = 512
TM_MERGE = 512
SB_PAGE_SLOTS = 16


def kernel(x_prompt, x_sample, cache_sb_k, cache_sb_v, state_conv, state_lru, cache_dil_k, cache_dil_v,
           page_table, rel_bias, norm_mix, norm_ffn, norm_final, w_in_ab, sb_bias, conv_w, conv_b,
           lru_wa, lru_ba, lru_wx, lru_bx, lru_lambda, w_out_ab, w_in_c, w_out_c, w_ff1, w_ff2):
    bp, s, d = x_prompt.shape
    bs, t_new, _ = x_sample.shape
    depth = norm_mix.shape[0]
    sbw = SB_HEADS * HEAD_DIM
    lw = conv_w.shape[2]
    dil_heads = w_in_c.shape[2] // 3 // HEAD_DIM
    dw = dil_heads * HEAD_DIM
    qscale = 1.0 / math.sqrt(HEAD_DIM)
    c_buf = cache_dil_k.shape[2]
    keep = min(DIL_MAX_WINDOW, s)

    tri = _suffix_matrix()
    tok_minor = lambda a: a.transpose(0, 1, 3, 4, 2)
    pool_k, pool_v = tok_minor(cache_sb_k), tok_minor(cache_sb_v)
    buf_k, buf_v = tok_minor(cache_dil_k), tok_minor(cache_dil_v)
    dil_bias_p = [_dil_prompt_bias(rel_bias, dil) for _, dil in DIL_PAIRS]
    dil_bias_s = _dil_sample_bias(rel_bias, t_new, c_buf, dil_heads, 2)
    lane_to_head = np.arange(LANES)[:, None] == (np.arange(dw) // HEAD_DIM)[None, :]
    expand = jnp.asarray(lane_to_head.astype(np.float32), dtype=BF16)

    hp = x_prompt.reshape(bp * s, d)
    hs = x_sample.reshape(bs * t_new, d)
    outs = {name: [] for name in ("sbk_p", "sbv_p", "sbk_s", "sbv_s", "conv_p", "conv_s", "lru_p", "lru_s",
                                  "dk_p", "dv_p", "dk_s", "dv_s")}
    y_prompt = y_sample = None
    for layer in range(depth):
        i = layer // 2
        last = layer == depth - 1
        g_fin = norm_final if last else None
        w1 = w_ff1[layer].astype(BF16)
        w2 = w_ff2[layer].astype(BF16)
        if layer % 2 == 0:
            w_in = w_in_ab[i].astype(BF16)
            w_out = w_out_ab[i].astype(BF16)
            wo_parts = [w_out[:sbw], w_out[sbw:]]
            wa = _block_diag_weights(lru_wa[i])
            wx = _block_diag_weights(lru_wx[i])
            lru_args = (conv_w[i], conv_b[i], wa, lru_ba[i], wx, lru_bx[i], lru_lambda[i])
            bias_rows = jnp.broadcast_to(sb_bias[i].astype(F32)[:, None], (SB_HEADS, LANES))
            spec = [(0, sbw, qscale), (sbw, sbw, 1.0), (sbw, sbw, 1.0), (2 * sbw, sbw, 1.0), (2 * sbw, sbw, 1.0),
                    (3 * sbw, lw, 1.0), (3 * sbw + lw, lw, 1.0)]
            qb, k32, kb, v32, vb, xr, gate = _norm_proj(hp, norm_mix[layer], w_in, spec,
                                                        [BF16, F32, BF16, F32, BF16, F32, F32], TM_PROJ)
            r3 = lambda a: a.reshape(bp, s, a.shape[1])
            attn = _sb_prompt(r3(qb), r3(kb), r3(vb), bias_rows, tri, T_SB)
            lru_o, h_last = _lru_prompt(r3(xr), r3(gate), *lru_args, TS_LRU)
            outs["sbk_p"].append(k32.reshape(bp, s, SB_HEADS, HEAD_DIM))
            outs["sbv_p"].append(v32.reshape(bp, s, SB_HEADS, HEAD_DIM))
            outs["conv_p"].append(r3(xr)[:, s - (CONV_W - 1):])
            outs["lru_p"].append(h_last.reshape(bp, lw))
            hp = _out_mlp(hp, [attn.reshape(bp * s, sbw), lru_o.reshape(bp * s, lw)], wo_parts,
                          norm_ffn[layer], w1, w2, g_fin, TM_MLP, TF_MLP)
            spec_s = [(0, sbw, 1.0), (sbw, sbw, 1.0), (2 * sbw, sbw, 1.0), (3 * sbw, lw, 1.0), (3 * sbw + lw, lw, 1.0)]
            q_s, k_s, v_s, xr_s, g_s = _norm_proj(hs, norm_mix[layer], w_in, spec_s, [F32] * 5, TM_PROJ)
            r3s = lambda a: a.reshape(bs, t_new, a.shape[1])
            attn_s = _sb_sample(r3s(q_s), r3s(k_s), r3s(v_s), pool_k, pool_v, i, page_table, sb_bias[i],
                                tri, SB_PAGE_SLOTS)
            tmaj = lambda a: a.transpose(1, 0, 2)
            xr_s3 = r3s(xr_s)
            lru_os, h_last_s = _lru_sample(tmaj(xr_s3), tmaj(r3s(g_s)), tmaj(state_conv[i]), state_lru[i],
                                           *lru_args)
            outs["sbk_s"].append(k_s.reshape(bs, t_new, SB_HEADS, HEAD_DIM))
            outs["sbv_s"].append(v_s.reshape(bs, t_new, SB_HEADS, HEAD_DIM))
            conv_cat = jnp.concatenate([state_conv[i].astype(F32), xr_s3], axis=1)
            outs["conv_s"].append(conv_cat[:, t_new:])
            outs["lru_s"].append(h_last_s)
            hs = _out_mlp(hs, [attn_s, tmaj(lru_os).reshape(bs * t_new, lw)], wo_parts,
                          norm_ffn[layer], w1, w2, g_fin, TM_MLP, TF_MLP)
        else:
            w_in = w_in_c[i].astype(BF16)
            w_out = w_out_c[i].astype(BF16)
            dils = [dil for _, dil in DIL_PAIRS]
            qs, ks, vs, k32, v32 = _dil_proj(hp.reshape(bp, s, d), norm_mix[layer], w_in, dils, TM_PROJ)
            branches = [_dil_prompt_branch(q_, k_, v_, tab, dil)
                        for q_, k_, v_, tab, dil in zip(qs, ks, vs, dil_bias_p, dils)]
            o_p = _dil_merge(branches, expand, TM_MERGE)
            outs["dk_p"].append(k32[:, s - keep:].reshape(bp, keep, dil_heads, HEAD_DIM))
            outs["dv_p"].append(v32[:, s - keep:].reshape(bp, keep, dil_heads, HEAD_DIM))
            hp = _out_mlp(hp, [o_p], [w_out], norm_ffn[layer], w1, w2, g_fin, TM_MLP, TF_MLP)
            spec_s = [(0, dw, 1.0), (dw, dw, 1.0), (2 * dw, dw, 1.0)]
            q_s, k_s, v_s = _norm_proj(hs, norm_mix[layer], w_in, spec_s, [F32] * 3, TM_PROJ)
            r3s = lambda a: a.reshape(bs, t_new, a.shape[1])
            o_s = _dil_sample(r3s(q_s), r3s(k_s), r3s(v_s), buf_k, buf_v, i, dil_bias_s)
            outs["dk_s"].append(k_s.reshape(bs, t_new, dil_heads, HEAD_DIM))
            outs["dv_s"].append(v_s.reshape(bs, t_new, dil_heads, HEAD_DIM))
            hs = _out_mlp(hs, [o_s], [w_out], norm_ffn[layer], w1, w2, g_fin, TM_MLP, TF_MLP)
    y_prompt = hp.reshape(bp, s, d)
    y_sample = hs.reshape(bs, t_new, d)
    st = lambda name: jnp.stack(outs[name])
    return (y_prompt, y_sample, st("sbk_p"), st("sbv_p"), st("sbk_s"), st("sbv_s"),
            st("conv_p"), st("conv_s"), st("lru_p"), st("lru_s"),
            st("dk_p"), st("dv_p"), st("dk_s"), st("dv_s"))
```

```python
import functools
import math

import numpy as np
import jax
import jax.numpy as jnp
from jax import lax
from jax.experimental import pallas as pl
from jax.experimental.pallas import tpu as pltpu

F32 = jnp.float32
BF16 = jnp.bfloat16

HEAD_DIM = 64
SB_HEADS = 8
LRU_BLOCKS = 8
CONV_W = 4
RGLRU_C = 8.0
DIL_PAIRS = ((128, 1), (512, 4), (2048, 16))
DIL_MAX_WINDOW = 2048
DIL_BACK = 128
REL_BUCKETS = 32
REL_MAX_DIST = 2048
NORM_EPS = 1e-6
NEG = -1e30
LOG2E = 1.4426950408889634
LANES = 128
HEAD_GROUP = 4
GROUP_W = HEAD_GROUP * HEAD_DIM
VMEM_LIMIT = 56 * 1024 * 1024


def _cparams(*sem):
    return pltpu.CompilerParams(dimension_semantics=sem, vmem_limit_bytes=VMEM_LIMIT)


def _softplus(z):
    neg_abs = lax.bitcast_convert_type(lax.bitcast_convert_type(z, jnp.uint32) | jnp.uint32(0x80000000), F32)
    return jnp.maximum(z, 0.0) + jnp.log(1.0 + jnp.exp2(neg_abs * LOG2E))


def _sigmoid(z):
    return 1.0 / (1.0 + jnp.exp(-z))


def _gelu_tanh(x):
    c = math.sqrt(2.0 / math.pi)
    return 0.5 * x * (1.0 + jnp.tanh(c * (x + 0.044715 * (x * x * x))))


def _rms(x, g):
    inv = lax.rsqrt(jnp.mean(x * x, axis=-1, keepdims=True) + NORM_EPS)
    return x * inv * g


def _dot(a, b):
    return jnp.dot(a, b, preferred_element_type=F32)


def _dot_nt(a, b):
    return lax.dot_general(a, b, (((1,), (1,)), ((), ())), preferred_element_type=F32)


def _pad_rows_bf16(x, rows):
    pad = jnp.zeros((rows - x.shape[0], x.shape[1]), F32)
    return jnp.concatenate([x, pad], axis=0).astype(BF16)


def _suffix_matrix():
    j = np.arange(LANES)[:, None]
    s = np.arange(LANES)[None, :]
    strict = (j > s).astype(np.float32)
    zero = np.zeros_like(strict)
    return jnp.asarray(np.block([[strict, zero], [zero, strict]]), dtype=BF16)


def _norm_proj_kernel(x_ref, g_ref, w_ref, *out_refs, outs):
    xn = _rms(x_ref[...], g_ref[...]).astype(BF16)
    cache = {}
    for o_ref, (off, width, scale, transposed) in zip(out_refs, outs):
        if (off, width) not in cache:
            cache[off, width] = _dot(xn, w_ref[:, off:off + width])
        z = cache[off, width]
        if scale != 1.0:
            z = z * scale
        o_ref[...] = (z.T if transposed else z).astype(o_ref.dtype)


def _norm_proj(x, g, w, outs, dtypes, tm):
    b, s, d = x.shape
    tm = min(tm, s)
    kern = functools.partial(_norm_proj_kernel, outs=tuple(outs))
    spec = lambda wd, tr: (pl.BlockSpec((None, wd, tm), lambda b_, i: (b_, 0, i)) if tr
                           else pl.BlockSpec((None, tm, wd), lambda b_, i: (b_, i, 0)))
    return pl.pallas_call(
        kern,
        grid=(b, s // tm),
        in_specs=[pl.BlockSpec((None, tm, d), lambda b_, i: (b_, i, 0)),
                  pl.BlockSpec((1, d), lambda b_, i: (0, 0)),
                  pl.BlockSpec(w.shape, lambda b_, i: (0, 0))],
        out_specs=[spec(wd, tr) for (_, wd, _, tr) in outs],
        out_shape=[jax.ShapeDtypeStruct((b, wd, s) if tr else (b, s, wd), dt)
                   for (_, wd, _, tr), dt in zip(outs, dtypes)],
        compiler_params=_cparams("parallel", "parallel"),
        name="norm_proj",
    )(x, g.reshape(1, d), w)


def _dil_proj_kernel(x_ref, g_ref, w_ref, *refs, dils, qscale, first_kept):
    n = len(dils)
    q_refs, k_refs, v_refs = refs[:n], refs[n:2 * n], refs[2 * n:3 * n]
    kt_ref, vt_ref, zs_ref = refs[3 * n:]
    tm = x_ref.shape[0]
    wd = w_ref.shape[1] // 3
    xn = _rms(x_ref[...], g_ref[...]).astype(BF16)
    for idx, (o_refs, tail_ref, scale) in enumerate(((q_refs, None, qscale), (k_refs, kt_ref, 1.0),
                                                     (v_refs, vt_ref, 1.0))):
        z = _dot(xn, w_ref[:, idx * wd:(idx + 1) * wd])
        if tail_ref is not None:
            @pl.when(pl.program_id(1) >= first_kept)
            def _():
                tail_ref[...] = z.T
        for c in range(wd // LANES):
            zs_ref[c] = z[:, c * LANES:(c + 1) * LANES] * scale
        for o_ref, d in zip(o_refs, dils):
            for r in range(d):
                for c in range(wd // LANES):
                    o_ref[r, :, c * LANES:(c + 1) * LANES] = (
                        zs_ref[c, pl.ds(r, tm // d, stride=d), :].astype(BF16))


def _dil_proj(x, g, w, dils, keep, tm):
    b, s, d_model = x.shape
    wd = w.shape[1] // 3
    tm = min(tm, s, keep)
    assert keep % tm == 0
    n = len(dils)
    first_kept = (s - keep) // tm
    res_spec = lambda d: pl.BlockSpec((None, d, tm // d, wd), lambda b_, i: (b_, 0, i, 0))
    res_shape = lambda d: jax.ShapeDtypeStruct((b, d, s // d, wd), BF16)
    tail_spec = pl.BlockSpec((None, wd, tm), lambda b_, i: (b_, 0, jnp.maximum(i - first_kept, 0)))
    outs = pl.pallas_call(
        functools.partial(_dil_proj_kernel, dils=tuple(dils), qscale=1.0 / math.sqrt(HEAD_DIM),
                          first_kept=first_kept),
        grid=(b, s // tm),
        in_specs=[pl.BlockSpec((None, tm, d_model), lambda b_, i: (b_, i, 0)),
                  pl.BlockSpec((1, d_model), lambda b_, i: (0, 0)),
                  pl.BlockSpec(w.shape, lambda b_, i: (0, 0))],
        out_specs=[res_spec(d) for d in dils] * 3 + [tail_spec, tail_spec],
        out_shape=[res_shape(d) for d in dils] * 3 + [jax.ShapeDtypeStruct((b, wd, keep), F32)] * 2,
        scratch_shapes=[pltpu.VMEM((wd // LANES, tm, LANES), F32)],
        compiler_params=_cparams("parallel", "arbitrary"),
        name="dil_proj",
    )(x, g.reshape(1, d_model), w)
    return outs[:n], outs[n:2 * n], outs[2 * n:3 * n], outs[3 * n], outs[3 * n + 1]


def _out_mlp_kernel(*refs, n_parts, final):
    h_ref = refs[0]
    part_refs = refs[1:1 + n_parts]
    wo_refs = refs[1 + n_parts:1 + 2 * n_parts]
    rest = refs[1 + 2 * n_parts:]
    if final:
        gf_ref, w1_ref, w2_ref, gfin_ref, o_ref, hn_ref, xn_ref, acc_ref = rest
    else:
        gf_ref, w1_ref, w2_ref, o_ref, hn_ref, xn_ref, acc_ref = rest
    f = pl.program_id(1)

    @pl.when(f == 0)
    def _():
        hn = h_ref[...]
        for p_ref, wo_ref in zip(part_refs, wo_refs):
            hn = hn + _dot(p_ref[...].astype(BF16), wo_ref[...])
        hn_ref[...] = hn
        xn_ref[...] = _rms(hn, gf_ref[...]).astype(BF16)
        acc_ref[...] = jnp.zeros_like(acc_ref)

    u = jnp.maximum(_dot(xn_ref[...], w1_ref[...]), 0.0)
    acc_ref[...] += _dot((u * u).astype(BF16), w2_ref[...])

    @pl.when(f == pl.num_programs(1) - 1)
    def _():
        out = hn_ref[...] + acc_ref[...]
        if final:
            out = _rms(out, gfin_ref[...])
        o_ref[...] = out


def _out_mlp(h, parts, wouts, g_ffn, w1, w2, g_final, tm, tf):
    m, d = h.shape
    dff = w1.shape[1]
    tm = min(tm, m)
    final = g_final is not None
    n_parts = len(parts)
    in_specs = [pl.BlockSpec((tm, d), lambda i, f: (i, 0))]
    in_specs += [pl.BlockSpec((tm, p.shape[1]), lambda i, f: (i, 0)) for p in parts]
    in_specs += [pl.BlockSpec(wo.shape, lambda i, f: (0, 0)) for wo in wouts]
    in_specs += [pl.BlockSpec((1, d), lambda i, f: (0, 0)),
                 pl.BlockSpec((d, tf), lambda i, f: (0, f)),
                 pl.BlockSpec((tf, d), lambda i, f: (f, 0))]
    args = [h, *parts, *wouts, g_ffn.reshape(1, d), w1, w2]
    if final:
        in_specs.append(pl.BlockSpec((1, d), lambda i, f: (0, 0)))
        args.append(g_final.reshape(1, d))
    kern = functools.partial(_out_mlp_kernel, n_parts=n_parts, final=final)
    return pl.pallas_call(
        kern,
        grid=(m // tm, dff // tf),
        in_specs=in_specs,
        out_specs=pl.BlockSpec((tm, d), lambda i, f: (i, 0)),
        out_shape=jax.ShapeDtypeStruct((m, d), F32),
        scratch_shapes=[pltpu.VMEM((tm, d), F32), pltpu.VMEM((tm, d), BF16), pltpu.VMEM((tm, d), F32)],
        compiler_params=_cparams("parallel", "arbitrary"),
        name="out_mlp",
    )(*args)


def _head_lane_mask(shape, h):
    lane = lax.broadcasted_iota(jnp.int32, shape, len(shape) - 1)
    return (lane >= h * HEAD_DIM) & (lane < (h + 1) * HEAD_DIM)


def _sb_weights(zs, r, tri):
    rows = zs[0].shape[0]
    sps = [_softplus(z) for z in zs]
    pairs = [jnp.concatenate([sps[i + 1].astype(BF16), sps[i].astype(BF16)], axis=1) for i in range(0, len(zs), 2)]
    rest = _dot(jnp.concatenate(pairs, axis=0), tri)
    ws = []
    for i, (z, sp) in enumerate(zip(zs, sps)):
        part = rest[(i // 2) * rows:(i // 2 + 1) * rows]
        later = part[:, LANES:] if i % 2 == 0 else part[:, :LANES]
        ws.append(jnp.exp(z - sp - later - r))
        r = r + jnp.sum(sp, axis=1, keepdims=True)
    return ws, r


def _sb_prompt_kernel(it_ref, jt_ref, q_ref, k_ref, v_ref, bias_ref, tri_ref, o_ref,
                      qm_ref, r_ref, acc_ref, *, t):
    g = pl.program_id(1)
    p = pl.program_id(2)
    i = it_ref[p]
    j = jt_ref[p]

    @pl.when(j == i)
    def _():
        q = q_ref[...]
        for h in range(HEAD_GROUP):
            qm_ref[h] = jnp.where(_head_lane_mask(q.shape, h), q, jnp.zeros_like(q))
        r_ref[...] = jnp.zeros_like(r_ref)
        acc_ref[...] = jnp.zeros_like(acc_ref)

    def run(diag):
        span = 2 * LANES
        for h in range(HEAD_GROUP):
            qh = qm_ref[h]
            bias = bias_ref[pl.ds(g * HEAD_GROUP + h, 1), :]
            r = r_ref[h]
            acc = acc_ref[h]
            for half in reversed(range(t // span)):
                keys = slice(half * span, (half + 1) * span)
                s = _dot_nt(qh, k_ref[keys, :])
                zs = []
                for c in reversed(range(span // LANES)):
                    z = s[:, c * LANES:(c + 1) * LANES] + bias
                    if diag:
                        row = lax.broadcasted_iota(jnp.int32, z.shape, 0)
                        col = lax.broadcasted_iota(jnp.int32, z.shape, 1) + (half * span + c * LANES)
                        z = jnp.where(col < row, z, NEG)
                    zs.append(z)
                ws, r = _sb_weights(zs, r, tri_ref[...])
                w = jnp.concatenate([x.astype(BF16) for x in reversed(ws)], axis=1)
                acc = acc + _dot(w, v_ref[keys, :])
            r_ref[h] = r
            acc_ref[h] = acc

    @pl.when(j == i)
    def _():
        run(True)

    @pl.when(j != i)
    def _():
        run(False)

    @pl.when(j == 0)
    def _():
        out = jnp.zeros(o_ref.shape, F32)
        for h in range(HEAD_GROUP):
            a = acc_ref[h]
            out = jnp.where(_head_lane_mask(a.shape, h), a, out)
        o_ref[...] = out.astype(o_ref.dtype)


def _sb_prompt(q, k, v, bias_rows, tri, t):
    b, s, w = q.shape
    t = min(t, s)
    nq = s // t
    it = np.concatenate([np.full(i + 1, i) for i in range(nq)]).astype(np.int32)
    jt = np.concatenate([np.arange(i, -1, -1) for i in range(nq)]).astype(np.int32)
    grid_spec = pltpu.PrefetchScalarGridSpec(
        num_scalar_prefetch=2,
        grid=(b, w // GROUP_W, len(it)),
        in_specs=[pl.BlockSpec((None, t, GROUP_W), lambda b_, g, p, it_, jt_: (b_, it_[p], g)),
                  pl.BlockSpec((None, t, GROUP_W), lambda b_, g, p, it_, jt_: (b_, jt_[p], g)),
                  pl.BlockSpec((None, t, GROUP_W), lambda b_, g, p, it_, jt_: (b_, jt_[p], g)),
                  pl.BlockSpec(bias_rows.shape, lambda b_, g, p, it_, jt_: (0, 0)),
                  pl.BlockSpec(tri.shape, lambda b_, g, p, it_, jt_: (0, 0))],
        out_specs=pl.BlockSpec((None, t, GROUP_W), lambda b_, g, p, it_, jt_: (b_, it_[p], g)),
        scratch_shapes=[pltpu.VMEM((HEAD_GROUP, t, GROUP_W), BF16),
                        pltpu.VMEM((HEAD_GROUP, t, LANES), F32),
                        pltpu.VMEM((HEAD_GROUP, t, GROUP_W), F32)])
    return pl.pallas_call(
        functools.partial(_sb_prompt_kernel, t=t),
        grid_spec=grid_spec,
        out_shape=jax.ShapeDtypeStruct((b, s, w), BF16),
        compiler_params=_cparams("parallel", "parallel", "arbitrary"),
        name="sb_prompt",
    )(jnp.asarray(it), jnp.asarray(jt), q, k, v, bias_rows, tri)


def _block_diag_queries(q, heads, halves):
    b, t, w = q.shape
    hh = heads // halves
    wh = w // halves
    qs = (q * (1.0 / math.sqrt(HEAD_DIM))).reshape(b, t, halves, 1, wh)
    lane_head = (np.arange(wh) // HEAD_DIM)[None, :]
    mask = jnp.asarray(lane_head == np.arange(hh)[:, None])
    qbd = jnp.where(mask[None, None, None], qs, 0.0)
    return qbd.transpose(0, 2, 1, 3, 4).reshape(b, halves, t * hh, wh).astype(BF16)


def _pad_new_rows(x):
    t = x.shape[1]
    return jnp.pad(x, ((0, 0), (0, -t % 8), (0, 0)))


def _store_head_rows(o_ref, x, n_new):
    heads = x.shape[0] // n_new
    shape = (heads, x.shape[1])
    lane_head = jnp.right_shift(lax.broadcasted_iota(jnp.int32, shape, 1), HEAD_DIM.bit_length() - 1)
    sel = lane_head == lax.broadcasted_iota(jnp.int32, shape, 0)
    for tk in range(n_new):
        row = jnp.sum(jnp.where(sel, x[tk * heads:(tk + 1) * heads], 0.0), axis=0, keepdims=True)
        o_ref[tk:tk + 1, :] = row.astype(o_ref.dtype)


def _sb_sample_kernel(*refs, n_slots, n_new):
    pt_ref, qbd_ref, kn_ref, vn_ref, bias_ref, tri_ref = refs[:6]
    kp_refs = refs[6:6 + n_slots]
    vp_refs = refs[6 + n_slots:6 + 2 * n_slots]
    o_ref, r_ref, acc_ref = refs[6 + 2 * n_slots:]
    del pt_ref
    c = pl.program_id(1)
    rows, w = qbd_ref.shape
    heads = rows // n_new
    qbd = qbd_ref[...]
    bias = bias_ref[...]
    tri = tri_ref[...]

    @pl.when(c == 0)
    def _():
        z = _dot_nt(qbd, _pad_rows_bf16(kn_ref[...], LANES)) + bias
        tok = jnp.right_shift(lax.broadcasted_iota(jnp.int32, z.shape, 0), heads.bit_length() - 1)
        z = jnp.where(lax.broadcasted_iota(jnp.int32, z.shape, 1) < tok, z, NEG)
        (wgt, _), r = _sb_weights([z, jnp.full_like(z, NEG)], jnp.zeros((rows, LANES), F32), tri)
        r_ref[...] = r
        acc_ref[...] = _dot(wgt.astype(BF16), _pad_rows_bf16(vn_ref[...], LANES))

    cat = lambda page_refs: jnp.concatenate([p[...].reshape(w, LANES).astype(BF16) for p in page_refs], axis=1)
    z = _dot(qbd, cat(kp_refs))
    ws, r = _sb_weights([z[:, p * LANES:(p + 1) * LANES] + bias for p in range(n_slots)], r_ref[...], tri)
    r_ref[...] = r
    acc = acc_ref[...] + _dot_nt(jnp.concatenate([x.astype(BF16) for x in ws], axis=1), cat(vp_refs))
    acc_ref[...] = acc

    @pl.when(c == pl.num_programs(1) - 1)
    def _():
        _store_head_rows(o_ref, acc, n_new)


def _sb_sample(q, k_new, v_new, pool_k, pool_v, layer, page_table, sb_bias, tri, n_slots):
    b, t, w = q.shape
    heads, hd, page = pool_k.shape[2:]
    n_pages = page_table.shape[1]
    assert page == LANES and n_pages % n_slots == 0 and heads & (heads - 1) == 0
    n_chunks = n_pages // n_slots
    rows = t * heads
    qbd = _block_diag_queries(q, heads, 1).reshape(b, rows, w)
    kn, vn = _pad_new_rows(k_new), _pad_new_rows(v_new)
    n_pad = kn.shape[1]
    bias_col = jnp.broadcast_to(jnp.tile(sb_bias.astype(F32), t)[:, None], (rows, LANES))

    def pool_spec(slot):
        return pl.BlockSpec(
            (None, None, heads, hd, page),
            lambda b_, c, pt: (layer, pt[b_, n_pages - 1 - (c * n_slots + slot)], 0, 0, 0))

    new_spec = pl.BlockSpec((None, n_pad, w), lambda b_, c, pt: (b_, 0, 0))
    grid_spec = pltpu.PrefetchScalarGridSpec(
        num_scalar_prefetch=1,
        grid=(b, n_chunks),
        in_specs=[pl.BlockSpec((None, rows, w), lambda b_, c, pt: (b_, 0, 0)), new_spec, new_spec,
                  pl.BlockSpec(bias_col.shape, lambda b_, c, pt: (0, 0)),
                  pl.BlockSpec(tri.shape, lambda b_, c, pt: (0, 0))]
                 + [pool_spec(sl) for sl in range(n_slots)] * 2,
        out_specs=pl.BlockSpec((None, t, w), lambda b_, c, pt: (b_, 0, 0)),
        scratch_shapes=[pltpu.VMEM((rows, LANES), F32), pltpu.VMEM((rows, w), F32)])
    out = pl.pallas_call(
        functools.partial(_sb_sample_kernel, n_slots=n_slots, n_new=t),
        grid_spec=grid_spec,
        out_shape=jax.ShapeDtypeStruct((b, t, w), F32),
        compiler_params=_cparams("parallel", "arbitrary"),
        name="sb_sample",
    )(page_table, qbd, kn, vn, bias_col, tri, *([pool_k] * n_slots), *([pool_v] * n_slots))
    return out.reshape(b * t, w)


def _lru_gates(xc, wa_ref, ba_ref, wx_ref, bx_ref, lam_ref):
    xcb = xc.astype(BF16)
    r = _sigmoid(_dot(xcb, wa_ref[...]) + ba_ref[...])
    ig = _sigmoid(_dot(xcb, wx_ref[...]) + bx_ref[...])
    log_a = (-RGLRU_C * _softplus(-lam_ref[...])) * r
    a = jnp.exp(log_a)
    inp = jnp.sqrt(-jnp.tanh(log_a) * (a * a + 1.0)) * (ig * xc)
    return a, inp


def _lru_prompt_kernel(xr_ref, g_ref, cw_ref, cb_ref, wa_ref, ba_ref, wx_ref, bx_ref, lam_ref,
                       o_ref, hl_ref, xp_ref, h_ref, *, ts):
    c = pl.program_id(1)
    pad = xp_ref.shape[0] - ts

    @pl.when(c == 0)
    def _():
        xp_ref[0:pad, :] = jnp.zeros((pad, xp_ref.shape[1]), F32)
        h_ref[...] = jnp.zeros_like(h_ref)

    x = xr_ref[...]
    xp_ref[pad:pad + ts, :] = x
    xc = cb_ref[...]
    for jj in range(CONV_W):
        sh = CONV_W - 1 - jj
        xc = xc + cw_ref[jj:jj + 1, :] * xp_ref[pad - sh:pad - sh + ts, :]
    xp_ref[0:pad, :] = x[ts - pad:ts, :]

    a, bt = _lru_gates(xc, wa_ref, ba_ref, wx_ref, bx_ref, lam_ref)
    row = lax.broadcasted_iota(jnp.int32, a.shape, 0)
    d = 1
    while d < ts:
        keep = row >= d
        bt = jnp.where(keep, a * pltpu.roll(bt, d, 0) + bt, bt)
        a = jnp.where(keep, a * pltpu.roll(a, d, 0), a)
        d *= 2
    hs = a * h_ref[...] + bt
    h_ref[...] = hs[ts - 1:ts, :]
    hl_ref[...] = hs[ts - 1:ts, :]
    o_ref[...] = (_gelu_tanh(g_ref[...]) * hs).astype(o_ref.dtype)


def _lru_prompt(xr, g, cw, cb, wa, ba, wx, bx, lam, ts):
    b, s, w = xr.shape
    ts = min(ts, s)
    row = lambda a: a.reshape(1, w)
    vec = pl.BlockSpec((1, w), lambda b_, c: (0, 0))
    mat = pl.BlockSpec((w, w), lambda b_, c: (0, 0))
    seq = pl.BlockSpec((None, ts, w), lambda b_, c: (b_, c, 0))
    return pl.pallas_call(
        functools.partial(_lru_prompt_kernel, ts=ts),
        grid=(b, s // ts),
        in_specs=[seq, seq, pl.BlockSpec((CONV_W, w), lambda b_, c: (0, 0)), vec, mat, vec, mat, vec, vec],
        out_specs=[seq, pl.BlockSpec((None, 1, w), lambda b_, c: (b_, 0, 0))],
        out_shape=[jax.ShapeDtypeStruct((b, s, w), BF16), jax.ShapeDtypeStruct((b, 1, w), F32)],
        scratch_shapes=[pltpu.VMEM((ts + 8, w), F32), pltpu.VMEM((1, w), F32)],
        compiler_params=_cparams("parallel", "arbitrary"),
        name="lru_prompt",
    )(xr, g, cw, row(cb), wa, row(ba), wx, row(bx), row(lam))


def _lru_sample_kernel(xr_ref, g_ref, st_ref, h0_ref, cw_ref, cb_ref, wa_ref, ba_ref, wx_ref, bx_ref,
                       lam_ref, o_ref, hl_ref):
    n_t = xr_ref.shape[0]
    n_s = st_ref.shape[0]
    xs = [st_ref[i] for i in range(n_s)] + [xr_ref[i] for i in range(n_t)]
    h = h0_ref[...]
    for tt in range(n_t):
        xc = cb_ref[...]
        for jj in range(CONV_W):
            xc = xc + cw_ref[jj:jj + 1, :] * xs[tt + jj]
        a, inp = _lru_gates(xc, wa_ref, ba_ref, wx_ref, bx_ref, lam_ref)
        h = a * h + inp
        o_ref[tt] = (_gelu_tanh(g_ref[tt]) * h).astype(o_ref.dtype)
    hl_ref[...] = h


def _lru_sample(xr_t, g_t, state_t, h0, cw, cb, wa, ba, wx, bx, lam):
    n_t, b, w = xr_t.shape
    row = lambda a: a.reshape(1, w)
    return pl.pallas_call(
        _lru_sample_kernel,
        out_shape=[jax.ShapeDtypeStruct((n_t, b, w), BF16), jax.ShapeDtypeStruct((b, w), F32)],
        compiler_params=pltpu.CompilerParams(vmem_limit_bytes=VMEM_LIMIT),
        name="lru_sample",
    )(xr_t, g_t, state_t, h0, cw, row(cb), wa, row(ba), wx, row(bx), row(lam))


def _rel_bucket_np(dist):
    exact = REL_BUCKETS // 2
    df = np.maximum(dist.astype(np.float64), 1.0)
    large = exact + (np.log(df / exact) / math.log(REL_MAX_DIST / exact) * (REL_BUCKETS - exact)).astype(np.int64)
    large = np.minimum(large, REL_BUCKETS - 1)
    return np.where(dist < exact, dist, large)


def _dil_prompt_kernel(q_ref, kp_ref, kc_ref, vp_ref, vc_ref, bias_ref, o_ref, m_ref, l_ref, *, dil):
    tq = q_ref.shape[0]
    n_groups = q_ref.shape[1] // GROUP_W
    res = pl.program_id(2)
    kcol = lax.broadcasted_iota(jnp.int32, (HEAD_GROUP * tq, 2 * tq), 1)
    prev_pen = jnp.where(kcol < tq, jnp.where(pl.program_id(1) == 0, NEG, 0.0), 0.0)
    lane = lax.broadcasted_iota(jnp.int32, (tq, LANES), 1)
    m_all = jnp.zeros((tq, LANES), F32)
    l_all = jnp.ones((tq, LANES), F32)
    rows = slice(None) if dil == 1 else pl.ds(res, tq, stride=dil)
    for g in range(n_groups):
        cols = slice(g * GROUP_W, (g + 1) * GROUP_W)
        q = q_ref[:, cols]
        kk = jnp.concatenate([kp_ref[:, cols], kc_ref[:, cols]], axis=0)
        vv = jnp.concatenate([vp_ref[:, cols], vc_ref[:, cols]], axis=0)
        masks = [_head_lane_mask(q.shape, h) for h in range(HEAD_GROUP)]
        qst = jnp.concatenate([jnp.where(hm, q, jnp.zeros_like(q)) for hm in masks], axis=0)
        s = _dot_nt(qst, kk) + bias_ref[g] + prev_pen
        m = jnp.max(s, axis=1, keepdims=True)
        p = jnp.exp(s - m)
        l = jnp.sum(p, axis=1, keepdims=True)
        ost = _dot(p.astype(BF16), vv)
        out = jnp.zeros((tq, GROUP_W), F32)
        for h in range(HEAD_GROUP):
            part = slice(h * tq, (h + 1) * tq)
            out = jnp.where(masks[h], ost[part], out)
            hh = g * HEAD_GROUP + h
            m_all = jnp.where(lane == hh, m[part], m_all)
            l_all = jnp.where(lane == hh, l[part], l_all)
        for j in range(GROUP_W // LANES):
            o_ref[g * (GROUP_W // LANES) + j, rows, :] = out[:, j * LANES:(j + 1) * LANES]
    m_ref[rows, :] = m_all
    l_ref[rows, :] = l_all


def _dil_prompt_branch(q, k, v, bias_tab, dil):
    b, _, length, w = q.shape
    s = length * dil
    tq = DIL_BACK
    n_blk = length // tq
    cur = pl.BlockSpec((None, None, tq, w), lambda b_, i, r: (b_, r, i, 0))
    prev = pl.BlockSpec((None, None, tq, w), lambda b_, i, r: (b_, r, jnp.maximum(i - 1, 0), 0))
    stat = pl.BlockSpec((None, tq * dil, LANES), lambda b_, i, r: (b_, i, 0))
    return pl.pallas_call(
        functools.partial(_dil_prompt_kernel, dil=dil),
        grid=(b, n_blk, dil),
        in_specs=[cur, prev, cur, prev, cur, pl.BlockSpec(bias_tab.shape, lambda b_, i, r: (0, 0, 0))],
        out_specs=[pl.BlockSpec((None, w // LANES, tq * dil, LANES), lambda b_, i, r: (b_, 0, i, 0)), stat, stat],
        out_shape=[jax.ShapeDtypeStruct((b, w // LANES, s, LANES), F32),
                   jax.ShapeDtypeStruct((b, s, LANES), F32),
                   jax.ShapeDtypeStruct((b, s, LANES), F32)],
        compiler_params=_cparams("parallel", "parallel", "arbitrary"),
        name="dil_prompt_d%d" % dil,
    )(q, k, k, v, v, bias_tab)


def _expand_heads(x, e_ref):
    hi = x.astype(BF16)
    r1 = x - hi.astype(F32)
    mid = r1.astype(BF16)
    lo = (r1 - mid.astype(F32)).astype(BF16)
    e = e_ref[...]
    return _dot(hi, e) + _dot(mid, e) + _dot(lo, e)


def _dil_merge_kernel(*refs, n_br):
    o_refs = refs[:n_br]
    m_refs = refs[n_br:2 * n_br]
    l_refs = refs[2 * n_br:3 * n_br]
    e_ref, out_ref = refs[3 * n_br:]
    ms = [r[...] for r in m_refs]
    m_max = functools.reduce(jnp.maximum, ms)
    ws = [jnp.exp(m - m_max) for m in ms]
    tot = functools.reduce(lambda x, y: x + y, [w * r[...] for w, r in zip(ws, l_refs)])
    inv = 1.0 / tot
    coefs = [_expand_heads(w * inv, e_ref) for w in ws]
    for c in range(o_refs[0].shape[0]):
        cols = slice(c * LANES, (c + 1) * LANES)
        acc = functools.reduce(lambda x, y: x + y, [cf[:, cols] * o_ref[c] for cf, o_ref in zip(coefs, o_refs)])
        out_ref[:, cols] = acc.astype(out_ref.dtype)


def _dil_merge(outs, expand, tm):
    n_br = len(outs)
    b, n_cb, s, _ = outs[0][0].shape
    w = n_cb * LANES
    tm = min(tm, s)
    n_t = s // tm
    wide = pl.BlockSpec((None, n_cb, tm, LANES), lambda b_, i: (b_, 0, i, 0))
    thin = pl.BlockSpec((None, tm, LANES), lambda b_, i: (b_, i, 0))
    return pl.pallas_call(
        functools.partial(_dil_merge_kernel, n_br=n_br),
        grid=(b, n_t),
        in_specs=[wide] * n_br + [thin] * (2 * n_br) + [pl.BlockSpec(expand.shape, lambda b_, i: (0, 0))],
        out_specs=pl.BlockSpec((tm, w), lambda b_, i: (b_ * n_t + i, 0)),
        out_shape=jax.ShapeDtypeStruct((b * s, w), BF16),
        compiler_params=_cparams("parallel", "parallel"),
        name="dil_merge",
    )(*[o for o, _, _ in outs], *[m for _, m, _ in outs], *[l for _, _, l in outs], expand)


def _dil_sample_kernel(qbd_ref, kb_ref, vb_ref, kn_ref, vn_ref, bias_ref, o_ref, *, n_new):
    wh, c_buf = kb_ref.shape[0] * kb_ref.shape[1], kb_ref.shape[2]
    qbd = qbd_ref[...]
    s = _dot(qbd, kb_ref[...].reshape(wh, c_buf).astype(BF16))
    sn = _dot_nt(qbd, _pad_rows_bf16(kn_ref[...], LANES))
    z = jnp.concatenate([s, sn], axis=1)
    logits = [z + bias_ref[i] for i in range(bias_ref.shape[0])]
    m = functools.reduce(jnp.maximum, [jnp.max(x, axis=1, keepdims=True) for x in logits])
    p = functools.reduce(lambda x, y: x + y, [jnp.exp(x - m) for x in logits])
    den = jnp.sum(p, axis=1, keepdims=True)
    pb = p.astype(BF16)
    o = _dot_nt(pb[:, :c_buf], vb_ref[...].reshape(wh, c_buf).astype(BF16))
    o = (o + _dot(pb[:, c_buf:], _pad_rows_bf16(vn_ref[...], LANES))) / den
    _store_head_rows(o_ref, o, n_new)


def _dil_sample_bias(rel_bias, n_new, c_buf, heads, halves):
    t = np.arange(n_new)[:, None]
    idx = np.arange(c_buf + LANES)[None, :]
    hh = heads // halves
    tabs = []
    for window, dil in DIL_PAIRS:
        dist = c_buf + t - idx
        valid = (dist >= 0) & (dist % dil == 0) & (dist // dil <= window // dil) & (idx < c_buf + n_new)
        bucket = _rel_bucket_np(np.clip(dist, 0, window))
        bias = rel_bias[jnp.asarray(bucket)].astype(F32)
        bias = jnp.where(jnp.asarray(valid)[:, :, None], bias, NEG)
        bias = bias.reshape(n_new, -1, halves, hh).transpose(2, 0, 3, 1)
        tabs.append(bias.reshape(halves, n_new * hh, -1))
    return jnp.stack(tabs)


def _dil_sample(q, k_new, v_new, buf_k, buf_v, layer, bias_tab):
    b, t, w = q.shape
    heads, hd, c_buf = buf_k.shape[2:]
    n_br, halves, rows, n_keys = bias_tab.shape
    hh, wh = heads // halves, w // halves
    qbd = _block_diag_queries(q, heads, halves)
    kn, vn = _pad_new_rows(k_new), _pad_new_rows(v_new)
    n_pad = kn.shape[1]
    buf_spec = pl.BlockSpec((None, None, hh, hd, c_buf), lambda b_, hf: (layer, b_, hf, 0, 0))
    new_spec = pl.BlockSpec((None, n_pad, wh), lambda b_, hf: (b_, 0, hf))
    out = pl.pallas_call(
        functools.partial(_dil_sample_kernel, n_new=t),
        grid=(b, halves),
        in_specs=[pl.BlockSpec((None, None, rows, wh), lambda b_, hf: (b_, hf, 0, 0)),
                  buf_spec, buf_spec, new_spec, new_spec,
                  pl.BlockSpec((n_br, None, rows, n_keys), lambda b_, hf: (0, hf, 0, 0))],
        out_specs=pl.BlockSpec((None, t, wh), lambda b_, hf: (b_, 0, hf)),
        out_shape=jax.ShapeDtypeStruct((b, t, w), F32),
        compiler_params=_cparams("parallel", "arbitrary"),
        name="dil_sample",
    )(qbd, buf_k, buf_v, kn, vn, bias_tab)
    return out.reshape(b * t, w)


def _dil_prompt_bias(rel_bias, dil):
    qi = np.arange(DIL_BACK)[:, None]
    kk = np.arange(2 * DIL_BACK)[None, :]
    j = qi + DIL_BACK - kk
    valid = (j >= 0) & (j <= DIL_BACK)
    bucket = _rel_bucket_np(np.clip(j, 0, DIL_BACK) * dil)
    onehot = jnp.asarray((bucket[..., None] == np.arange(REL_BUCKETS)).astype(np.float32))
    bias = jnp.einsum("qkb,bh->hqk", onehot, rel_bias.astype(F32), precision=lax.Precision.HIGHEST)
    bias = jnp.where(jnp.asarray(valid)[None], bias, NEG)
    return bias.reshape(bias.shape[0] // HEAD_GROUP, HEAD_GROUP * DIL_BACK, 2 * DIL_BACK)


def _block_diag_weights(wb):
    n, c, _ = wb.shape
    eye = jnp.asarray(np.eye(n, dtype=np.float32))
    return (wb[:, :, None, :] * eye[:, None, :, None]).reshape(n * c, n * c).astype(BF16)


TM_PROJ = 512
TM_MLP = 1024
TF_MLP = 1024
T_SB = 512
TS_LRU = 512
TM_MERGE = 512
SB_PAGE_SLOTS = 16


def kernel(x_prompt, x_sample, cache_sb_k, cache_sb_v, state_conv, state_lru, cache_dil_k, cache_dil_v,
           page_table, rel_bias, norm_mix, norm_ffn, norm_final, w_in_ab, sb_bias, conv_w, conv_b,
           lru_wa, lru_ba, lru_wx, lru_bx, lru_lambda, w_out_ab, w_in_c, w_out_c, w_ff1, w_ff2):
    bp, s, d = x_prompt.shape
    bs, t_new, _ = x_sample.shape
    depth = norm_mix.shape[0]
    sbw = SB_HEADS * HEAD_DIM
    lw = conv_w.shape[2]
    dil_heads = w_in_c.shape[2] // 3 // HEAD_DIM
    dw = dil_heads * HEAD_DIM
    qscale = 1.0 / math.sqrt(HEAD_DIM)
    c_buf = cache_dil_k.shape[2]
    keep = min(DIL_MAX_WINDOW, s)

    tri = _suffix_matrix()
    tok_minor = lambda a: a.transpose(0, 1, 3, 4, 2)
    pool_k, pool_v = tok_minor(cache_sb_k), tok_minor(cache_sb_v)
    buf_k, buf_v = tok_minor(cache_dil_k), tok_minor(cache_dil_v)
    dil_bias_p = [_dil_prompt_bias(rel_bias, dil) for _, dil in DIL_PAIRS]
    dil_bias_s = _dil_sample_bias(rel_bias, t_new, c_buf, dil_heads, 2)
    lane_to_head = np.arange(LANES)[:, None] == (np.arange(dw) // HEAD_DIM)[None, :]
    expand = jnp.asarray(lane_to_head.astype(np.float32), dtype=BF16)

    hp = x_prompt.reshape(bp * s, d)
    hs = x_sample.reshape(bs * t_new, d)
    outs = {name: [] for name in ("sbk_p", "sbv_p", "sbk_s", "sbv_s", "conv_p", "conv_s", "lru_p", "lru_s",
                                  "dk_p", "dv_p", "dk_s", "dv_s")}
    y_prompt = y_sample = None
    for layer in range(depth):
        i = layer // 2
        last = layer == depth - 1
        g_fin = norm_final if last else None
        w1 = w_ff1[layer].astype(BF16)
        w2 = w_ff2[layer].astype(BF16)
        if layer % 2 == 0:
            w_in = w_in_ab[i].astype(BF16)
            w_out = w_out_ab[i].astype(BF16)
            wo_parts = [w_out[:sbw], w_out[sbw:]]
            wa = _block_diag_weights(lru_wa[i])
            wx = _block_diag_weights(lru_wx[i])
            lru_args = (conv_w[i], conv_b[i], wa, lru_ba[i], wx, lru_bx[i], lru_lambda[i])
            bias_rows = jnp.broadcast_to(sb_bias[i].astype(F32)[:, None], (SB_HEADS, LANES))
            spec = [(0, sbw, qscale, False), (sbw, sbw, 1.0, True), (sbw, sbw, 1.0, False),
                    (2 * sbw, sbw, 1.0, True), (2 * sbw, sbw, 1.0, False),
                    (3 * sbw, lw, 1.0, False), (3 * sbw + lw, lw, 1.0, False)]
            qb, kt32, kb, vt32, vb, xr, gate = _norm_proj(hp.reshape(bp, s, d), norm_mix[layer], w_in, spec,
                                                          [BF16, F32, BF16, F32, BF16, F32, F32], TM_PROJ)
            attn = _sb_prompt(qb, kb, vb, bias_rows, tri, T_SB)
            lru_o, h_last = _lru_prompt(xr, gate, *lru_args, TS_LRU)
            outs["sbk_p"].append(kt32)
            outs["sbv_p"].append(vt32)
            outs["conv_p"].append(xr[:, s - (CONV_W - 1):])
            outs["lru_p"].append(h_last.reshape(bp, lw))
            hp = _out_mlp(hp, [attn.reshape(bp * s, sbw), lru_o.reshape(bp * s, lw)], wo_parts,
                          norm_ffn[layer], w1, w2, g_fin, TM_MLP, TF_MLP)
            spec_s = [(0, sbw, 1.0, False), (sbw, sbw, 1.0, False), (2 * sbw, sbw, 1.0, False),
                      (3 * sbw, lw, 1.0, False), (3 * sbw + lw, lw, 1.0, False)]
            q_s, k_s, v_s, xr_s, g_s = [a[0] for a in _norm_proj(hs[None], norm_mix[layer], w_in, spec_s,
                                                                 [F32] * 5, TM_PROJ)]
            r3s = lambda a: a.reshape(bs, t_new, a.shape[1])
            attn_s = _sb_sample(r3s(q_s), r3s(k_s), r3s(v_s), pool_k, pool_v, i, page_table, sb_bias[i],
                                tri, math.gcd(SB_PAGE_SLOTS, page_table.shape[1]))
            tmaj = lambda a: a.transpose(1, 0, 2)
            xr_s3 = r3s(xr_s)
            lru_os, h_last_s = _lru_sample(tmaj(xr_s3), tmaj(r3s(g_s)), tmaj(state_conv[i]), state_lru[i],
                                           *lru_args)
            outs["sbk_s"].append(k_s.reshape(bs, t_new, SB_HEADS, HEAD_DIM))
            outs["sbv_s"].append(v_s.reshape(bs, t_new, SB_HEADS, HEAD_DIM))
            conv_cat = jnp.concatenate([state_conv[i].astype(F32), xr_s3], axis=1)
            outs["conv_s"].append(conv_cat[:, t_new:])
            outs["lru_s"].append(h_last_s)
            hs = _out_mlp(hs, [attn_s, tmaj(lru_os).reshape(bs * t_new, lw)], wo_parts,
                          norm_ffn[layer], w1, w2, g_fin, TM_MLP, TF_MLP)
        else:
            w_in = w_in_c[i].astype(BF16)
            w_out = w_out_c[i].astype(BF16)
            dils = [dil for _, dil in DIL_PAIRS]
            qs, ks, vs, kt32, vt32 = _dil_proj(hp.reshape(bp, s, d), norm_mix[layer], w_in, dils, keep, TM_PROJ)
            branches = [_dil_prompt_branch(q_, k_, v_, tab, dil)
                        for q_, k_, v_, tab, dil in zip(qs, ks, vs, dil_bias_p, dils)]
            o_p = _dil_merge(branches, expand, TM_MERGE)
            outs["dk_p"].append(kt32)
            outs["dv_p"].append(vt32)
            hp = _out_mlp(hp, [o_p], [w_out], norm_ffn[layer], w1, w2, g_fin, TM_MLP, TF_MLP)
            spec_s = [(0, dw, 1.0, False), (dw, dw, 1.0, False), (2 * dw, dw, 1.0, False)]
            q_s, k_s, v_s = [a[0] for a in _norm_proj(hs[None], norm_mix[layer], w_in, spec_s, [F32] * 3, TM_PROJ)]
            r3s = lambda a: a.reshape(bs, t_new, a.shape[1])
            o_s = _dil_sample(r3s(q_s), r3s(k_s), r3s(v_s), buf_k, buf_v, i, dil_bias_s)
            outs["dk_s"].append(k_s.reshape(bs, t_new, dil_heads, HEAD_DIM))
            outs["dv_s"].append(v_s.reshape(bs, t_new, dil_heads, HEAD_DIM))
            hs = _out_mlp(hs, [o_s], [w_out], norm_ffn[layer], w1, w2, g_fin, TM_MLP, TF_MLP)
    y_prompt = hp.reshape(bp, s, d)
    y_sample = hs.reshape(bs, t_new, d)
    st = lambda name: jnp.stack(outs[name])

    def tok_major(name, heads):
        a = st(name)
        return a.reshape(a.shape[0], a.shape[1], heads, HEAD_DIM, a.shape[3]).transpose(0, 1, 4, 2, 3)

    return (y_prompt, y_sample, tok_major("sbk_p", SB_HEADS), tok_major("sbv_p", SB_HEADS), st("sbk_s"), st("sbv_s"),
            st("conv_p"), st("conv_s"), st("lru_p"), st("lru_s"),
            tok_major("dk_p", dil_heads), tok_major("dv_p", dil_heads), st("dk_s"), st("dv_s"))
```
